```python
import math
import jax, jax.numpy as jnp
from jax import lax
import numpy as np

D_MODEL = 2048
BATCH = 2
SEQ = 8192
DEPTH = 1

CHUNK = 64
Q_BLOCK = 128
EPS = 1e-6
NEG_INF = -1e30

MLA_HEADS = 16
MLA_Q_RANK = 768
MLA_KV_RANK = 512
MLA_NOPE = 128
MLA_ROPE = 64
MLA_V = 128
MLA_QK = MLA_NOPE + MLA_ROPE
ROPE_THETA = 10000.0

DIFF_HEADS = 8
DIFF_HEAD_DIM = 128
DIFF_V = 2 * DIFF_HEAD_DIM

N_BRANCH = 2
MLA_MIX = MLA_HEADS * MLA_V
DIFF_MIX = DIFF_HEADS * DIFF_V

D_FF = -(-8 * D_MODEL // (3 * 256)) * 256

IN_SIZES = (MLA_Q_RANK, MLA_KV_RANK, MLA_ROPE,
            DIFF_HEADS * 2 * DIFF_HEAD_DIM, DIFF_HEADS * 2 * DIFF_HEAD_DIM, DIFF_MIX,
            N_BRANCH * D_MODEL)
D_IN = sum(IN_SIZES)

kernel_name = "hybrid_mla_diffattn_gated_block"


def rmsnorm(x, g):
    xf = x.astype(jnp.float32)
    y = xf * lax.rsqrt(jnp.mean(xf * xf, axis=-1, keepdims=True) + EPS)
    return (y * g.astype(jnp.float32)).astype(x.dtype)


def rope_tables(pos, dim):
    inv = ROPE_THETA ** (-jnp.arange(0, dim, 2, dtype=jnp.float32) / dim)
    ang = pos.astype(jnp.float32)[:, None] * inv[None, :]
    return jnp.cos(ang), jnp.sin(ang)


def apply_rope(x, cos, sin):
    half = x.shape[-1] // 2
    xf = x.astype(jnp.float32)
    x1, x2 = xf[..., :half], xf[..., half:]
    c, s = cos[:, None, :], sin[:, None, :]
    return jnp.concatenate([x1 * c - x2 * s, x2 * c + x1 * s], axis=-1).astype(x.dtype)


def to_blocks(a):
    b, s = a.shape[:2]
    return jnp.moveaxis(a.reshape((b, s // Q_BLOCK, Q_BLOCK) + a.shape[2:]), 1, 0)


def from_blocks(a):
    nb, b, q = a.shape[:3]
    return jnp.moveaxis(a, 0, 1).reshape((b, nb * q) + a.shape[3:])


def chunk_allowed(q_pos, k_pos):
    return (k_pos[None, :] // CHUNK) <= (q_pos[:, None] // CHUNK)


def mla_branch(c_q, c_kv, k_pe, q_norm_g, w_uq, kv_norm_g, w_ukv, cos, sin):
    b, s, _ = c_q.shape
    q = (rmsnorm(c_q, q_norm_g) @ w_uq).reshape(b, s, MLA_HEADS, MLA_QK)
    q_nope = q[..., :MLA_NOPE]
    q_rope = apply_rope(q[..., MLA_NOPE:], cos, sin)
    kv = (rmsnorm(c_kv, kv_norm_g) @ w_ukv).reshape(b, s, MLA_HEADS, MLA_NOPE + MLA_V)
    k_nope, v = kv[..., :MLA_NOPE], kv[..., MLA_NOPE:]
    k_rope = apply_rope(k_pe[:, :, None, :], cos, sin)[:, :, 0, :]
    k_pos = jnp.arange(s)
    scale = 1.0 / math.sqrt(MLA_QK)

    def blk(args):
        qn, qr, bi = args
        q_pos = bi * Q_BLOCK + jnp.arange(Q_BLOCK)
        sc = (jnp.einsum('bqhd,bkhd->bhqk', qn, k_nope).astype(jnp.float32)
              + jnp.einsum('bqhr,bkr->bhqk', qr, k_rope).astype(jnp.float32)) * scale
        sc = jnp.where(chunk_allowed(q_pos, k_pos)[None, None], sc, NEG_INF)
        p = jax.nn.softmax(sc, axis=-1).astype(v.dtype)
        return jnp.einsum('bhqk,bkhd->bqhd', p, v)

    nb = s // Q_BLOCK
    o = lax.map(blk, (to_blocks(q_nope), to_blocks(q_rope), jnp.arange(nb)))
    return from_blocks(o).reshape(b, s, MLA_MIX)


def diff_branch(dq, dk, dv, lq1, lk1, lq2, lk2, sub_g, lambda_init, slopes):
    b, s, _ = dq.shape
    q = dq.reshape(b, s, DIFF_HEADS, 2, DIFF_HEAD_DIM)
    k = dk.reshape(b, s, DIFF_HEADS, 2, DIFF_HEAD_DIM)
    v = dv.reshape(b, s, DIFF_HEADS, DIFF_V)
    lam = (jnp.exp(jnp.sum(lq1.astype(jnp.float32) * lk1.astype(jnp.float32)))
           - jnp.exp(jnp.sum(lq2.astype(jnp.float32) * lk2.astype(jnp.float32)))
           + lambda_init)
    k_pos = jnp.arange(s)
    scale = 1.0 / math.sqrt(DIFF_HEAD_DIM)

    def blk(args):
        qb, bi = args
        q_pos = bi * Q_BLOCK + jnp.arange(Q_BLOCK)
        dist = jnp.abs(q_pos[:, None] - k_pos[None, :]).astype(jnp.float32)
        alibi = -slopes[:, None, None] * dist[None]
        sc = jnp.einsum('bqhmd,bkhmd->bhmqk', qb, k).astype(jnp.float32) * scale
        sc = sc + alibi[None, :, None]
        sc = jnp.where(chunk_allowed(q_pos, k_pos)[None, None, None], sc, NEG_INF)
        p = jax.nn.softmax(sc, axis=-1)
        a = (p[:, :, 0] - lam * p[:, :, 1]).astype(v.dtype)
        return jnp.einsum('bhqk,bkhd->bqhd', a, v)

    nb = s // Q_BLOCK
    o = from_blocks(lax.map(blk, (to_blocks(q), jnp.arange(nb))))
    o = rmsnorm(o, sub_g) * (1.0 - lambda_init)
    return o.reshape(b, s, DIFF_MIX)


def setup_inputs(seed: int = 0) -> dict:
    key = jax.random.key(seed)
    ks = jax.random.split(key, 24)

    def w(k, shape, fan_in):
        return jax.random.normal(k, shape, jnp.float32) * (fan_in ** -0.5)

    def gain(k, shape):
        return 1.0 + 0.02 * jax.random.normal(k, shape, jnp.float32)

    L = DEPTH
    return {
        "x": jax.random.normal(ks[0], (BATCH, SEQ, D_MODEL), jnp.float32),
        "attn_norm_g": gain(ks[1], (L, D_MODEL)),
        "w_in": w(ks[2], (L, D_MODEL, D_IN), D_MODEL),
        "b_gate": 0.01 * jax.random.normal(ks[3], (L, N_BRANCH * D_MODEL), jnp.float32),
        "q_norm_g": gain(ks[4], (L, MLA_Q_RANK)),
        "w_uq": w(ks[5], (L, MLA_Q_RANK, MLA_HEADS * MLA_QK), MLA_Q_RANK),
        "kv_norm_g": gain(ks[6], (L, MLA_KV_RANK)),
        "w_ukv": w(ks[7], (L, MLA_KV_RANK, MLA_HEADS * (MLA_NOPE + MLA_V)), MLA_KV_RANK),
        "lambda_q1": 0.1 * jax.random.normal(ks[8], (L, DIFF_HEAD_DIM), jnp.float32),
        "lambda_k1": 0.1 * jax.random.normal(ks[9], (L, DIFF_HEAD_DIM), jnp.float32),
        "lambda_q2": 0.1 * jax.random.normal(ks[10], (L, DIFF_HEAD_DIM), jnp.float32),
        "lambda_k2": 0.1 * jax.random.normal(ks[11], (L, DIFF_HEAD_DIM), jnp.float32),
        "diff_norm_g": gain(ks[12], (L, DIFF_V)),
        "w_mla_proj": w(ks[13], (L, MLA_MIX, D_MODEL), MLA_MIX),
        "w_diff_proj": w(ks[14], (L, DIFF_MIX, D_MODEL), DIFF_MIX),
        "w_out": w(ks[15], (L, D_MODEL, D_MODEL), D_MODEL),
        "ffn_norm_g": gain(ks[16], (L, D_MODEL)),
        "w_ffn_gate": w(ks[17], (L, D_MODEL, D_FF), D_MODEL),
        "w_ffn_up": w(ks[18], (L, D_MODEL, D_FF), D_MODEL),
        "w_ffn_down": w(ks[19], (L, D_FF, D_MODEL), D_FF),
        "final_norm_g": gain(ks[20], (D_MODEL,)),
    }


def reference(x, attn_norm_g, w_in, b_gate, q_norm_g, w_uq, kv_norm_g, w_ukv,
              lambda_q1, lambda_k1, lambda_q2, lambda_k2, diff_norm_g,
              w_mla_proj, w_diff_proj, w_out, ffn_norm_g, w_ffn_gate, w_ffn_up,
              w_ffn_down, final_norm_g):
    b, s, d = x.shape
    pos = jnp.arange(s)
    cos, sin = rope_tables(pos, MLA_ROPE)
    slopes = jnp.exp2(-8.0 * jnp.arange(1, DIFF_HEADS + 1, dtype=jnp.float32) / DIFF_HEADS)
    split_at = np.cumsum(IN_SIZES)[:-1].tolist()

    for i in range(DEPTH):
        lambda_init = 0.8 - 0.6 * math.exp(-0.3 * i)
        h = rmsnorm(x, attn_norm_g[i])
        proj = h @ w_in[i]
        c_q, c_kv, k_pe, dq, dk, dv, gate_logits = jnp.split(proj, split_at, axis=-1)

        y_mla = mla_branch(c_q, c_kv, k_pe, q_norm_g[i], w_uq[i], kv_norm_g[i], w_ukv[i], cos, sin)
        y_diff = diff_branch(dq, dk, dv, lambda_q1[i], lambda_k1[i], lambda_q2[i], lambda_k2[i],
                             diff_norm_g[i], lambda_init, slopes)

        g = jax.nn.sigmoid((gate_logits + b_gate[i]).astype(jnp.float32)).astype(x.dtype)
        g = g.reshape(b, s, N_BRANCH, d)
        merged = g[:, :, 0] * (y_mla @ w_mla_proj[i]) + g[:, :, 1] * (y_diff @ w_diff_proj[i])
        x = x + merged @ w_out[i]

        h2 = rmsnorm(x, ffn_norm_g[i])
        x = x + (jax.nn.silu(h2 @ w_ffn_gate[i]) * (h2 @ w_ffn_up[i])) @ w_ffn_down[i]

    return rmsnorm(x, final_norm_g)
```

```python
import functools
import math

import jax
import jax.numpy as jnp
from jax import lax
from jax.experimental import pallas as pl
from jax.experimental.pallas import tpu as pltpu

D_MODEL = 2048
CHUNK = 64
EPS = 1e-6
NEG_INF = -1e30
LOG2E = math.log2(math.e)

MLA_HEADS = 16
MLA_Q_RANK = 768
MLA_KV_RANK = 512
MLA_NOPE = 128
MLA_ROPE = 64
MLA_V = 128
MLA_QK = MLA_NOPE + MLA_ROPE
ROPE_THETA = 10000.0

DIFF_HEADS = 8
DIFF_HEAD_DIM = 128
DIFF_V = 2 * DIFF_HEAD_DIM

LANE = 128
MLA_QPAD = 2 * LANE

COL_CQ = 0
COL_KPE = 768
COL_CKV = 1024
COL_GATE = 2048
COL_DQ = COL_GATE + 2 * D_MODEL
COL_DK = COL_DQ + D_MODEL
COL_DV = COL_DK + D_MODEL
PROJ_COLS = COL_DV + D_MODEL

VMEM_LIMIT_BYTES = 56 * 1024 * 1024

BF16 = jnp.bfloat16
F32 = jnp.float32


def _params(n_axes):
    return pltpu.CompilerParams(dimension_semantics=("arbitrary",) * n_axes,
                                vmem_limit_bytes=VMEM_LIMIT_BYTES)


def _rmsnorm_rows(x, g):
    ms = jnp.mean(x * x, axis=-1, keepdims=True)
    return x * lax.rsqrt(ms + EPS) * g


def _norm_matmul_kernel(x_ref, g_ref, w_ref, o_ref, h_ref):
    @pl.when(pl.program_id(1) == 0)
    def _():
        h_ref[...] = _rmsnorm_rows(x_ref[...].astype(F32), g_ref[...]).astype(BF16)

    o_ref[...] = jnp.dot(h_ref[...], w_ref[...], preferred_element_type=F32).astype(o_ref.dtype)


def _norm_matmul(x, x_cols, x_col_block, g, w, tm, tn):
    m = x.shape[0]
    n = w.shape[1]
    return pl.pallas_call(
        _norm_matmul_kernel,
        grid=(m // tm, n // tn),
        in_specs=[pl.BlockSpec((tm, x_cols), lambda i, j: (i, x_col_block)),
                  pl.BlockSpec((1, x_cols), lambda i, j: (0, 0)),
                  pl.BlockSpec((x_cols, tn), lambda i, j: (0, j))],
        out_specs=pl.BlockSpec((tm, tn), lambda i, j: (i, j)),
        out_shape=jax.ShapeDtypeStruct((m, n), BF16),
        scratch_shapes=[pltpu.VMEM((tm, x_cols), BF16)],
        compiler_params=_params(2),
        name="norm_matmul",
    )(x, g.reshape(1, x_cols), w)


def _rope_block(blk, c1, c2):
    return blk * c1 + pltpu.roll(blk, LANE // 2, 1) * c2


def _q_proj_kernel(x_ref, g_ref, w_ref, c1_ref, c2_ref, o_ref, h_ref, *, heads_per_tile, qscale):
    @pl.when(pl.program_id(1) == 0)
    def _():
        h_ref[...] = _rmsnorm_rows(x_ref[...].astype(F32), g_ref[...]).astype(BF16)

    acc = jnp.dot(h_ref[...], w_ref[...], preferred_element_type=F32)
    c1 = c1_ref[...]
    c2 = c2_ref[...]
    for hh in range(heads_per_tile):
        lo = hh * MLA_QPAD
        o_ref[:, lo:lo + LANE] = (acc[:, lo:lo + LANE] * qscale).astype(o_ref.dtype)
        roped = _rope_block(acc[:, lo + LANE:lo + MLA_QPAD], c1, c2)
        o_ref[:, lo + LANE:lo + MLA_QPAD] = (roped * qscale).astype(o_ref.dtype)


def _q_proj(proj, g, w, c1, c2, seq, tm, tn):
    m = proj.shape[0]
    n = w.shape[1]
    s_tiles = seq // tm
    return pl.pallas_call(
        functools.partial(_q_proj_kernel, heads_per_tile=tn // MLA_QPAD,
                          qscale=LOG2E / math.sqrt(MLA_QK)),
        grid=(m // tm, n // tn),
        in_specs=[pl.BlockSpec((tm, MLA_Q_RANK), lambda i, j: (i, COL_CQ // MLA_Q_RANK)),
                  pl.BlockSpec((1, MLA_Q_RANK), lambda i, j: (0, 0)),
                  pl.BlockSpec((MLA_Q_RANK, tn), lambda i, j: (0, j)),
                  pl.BlockSpec((tm, LANE), lambda i, j: (i % s_tiles, 0)),
                  pl.BlockSpec((tm, LANE), lambda i, j: (i % s_tiles, 0))],
        out_specs=pl.BlockSpec((tm, tn), lambda i, j: (i, j)),
        out_shape=jax.ShapeDtypeStruct((m, n), BF16),
        scratch_shapes=[pltpu.VMEM((tm, MLA_Q_RANK), BF16)],
        compiler_params=_params(2),
        name="mla_q_proj",
    )(proj, g.reshape(1, MLA_Q_RANK), w, c1, c2)


def _k_rope_kernel(x_ref, c1_ref, c2_ref, o_ref):
    o_ref[...] = _rope_block(x_ref[...].astype(F32), c1_ref[...], c2_ref[...]).astype(o_ref.dtype)


def _k_rope(proj, c1, c2, seq, tm):
    m = proj.shape[0]
    s_tiles = seq // tm
    return pl.pallas_call(
        _k_rope_kernel,
        grid=(m // tm,),
        in_specs=[pl.BlockSpec((tm, LANE), lambda i: (i, COL_KPE // LANE)),
                  pl.BlockSpec((tm, LANE), lambda i: (i % s_tiles, 0)),
                  pl.BlockSpec((tm, LANE), lambda i: (i % s_tiles, 0))],
        out_specs=pl.BlockSpec((tm, LANE), lambda i: (i, 0)),
        out_shape=jax.ShapeDtypeStruct((m, LANE), BF16),
        compiler_params=_params(1),
        name="mla_k_rope",
    )(proj, c1, c2)


def _chunk_mask(t):
    qc = lax.shift_right_logical(lax.broadcasted_iota(jnp.int32, (t, t), 0), int(math.log2(CHUNK)))
    kc = lax.shift_right_logical(lax.broadcasted_iota(jnp.int32, (t, t), 1), int(math.log2(CHUNK)))
    return kc <= qc


def _online_softmax_step(s, v, m_sc, l_sc, acc_sc):
    m_prev = m_sc[...]
    m_new = jnp.maximum(m_prev, jnp.max(s, axis=1, keepdims=True))
    alpha = jnp.exp2(m_prev - m_new)
    p = jnp.exp2(s - m_new)
    l_sc[...] = alpha * l_sc[...] + jnp.sum(p, axis=1, keepdims=True)
    acc_sc[...] = alpha * acc_sc[...] + jnp.dot(p.astype(BF16), v, preferred_element_type=F32)
    m_sc[...] = m_new


def _mla_attn_kernel(q_ref, kn_ref, kr_ref, v_ref, o_ref, m_sc, l_sc, acc_sc, *, t):
    i = pl.program_id(2)
    q = q_ref[...]
    m_sc[...] = jnp.full(m_sc.shape, NEG_INF, F32)
    l_sc[...] = jnp.zeros(l_sc.shape, F32)
    acc_sc[...] = jnp.zeros(acc_sc.shape, F32)

    def step(j, masked):
        rows = pl.ds(pl.multiple_of(j * t, t), t)
        k = jnp.concatenate([kn_ref[rows, :], kr_ref[rows, :]], axis=1)
        s = lax.dot_general(q, k, (((1,), (1,)), ((), ())), preferred_element_type=F32)
        if masked:
            s = jnp.where(_chunk_mask(t), s, NEG_INF)
        _online_softmax_step(s, v_ref[rows, :], m_sc, l_sc, acc_sc)

    def body(j, carry):
        step(j, False)
        return carry

    lax.fori_loop(0, i, body, 0)
    step(i, True)
    o_ref[...] = (acc_sc[...] / l_sc[...]).astype(o_ref.dtype)


def _mla_attn(qm, kv, kr, batch, seq, t):
    nq = seq // t
    return pl.pallas_call(
        functools.partial(_mla_attn_kernel, t=t),
        grid=(batch, MLA_HEADS, nq),
        in_specs=[pl.BlockSpec((t, MLA_QPAD), lambda b, h, i: (b * nq + i, h)),
                  pl.BlockSpec((seq, LANE), lambda b, h, i: (b, h)),
                  pl.BlockSpec((seq, LANE), lambda b, h, i: (b, 0)),
                  pl.BlockSpec((seq, LANE), lambda b, h, i: (b, MLA_HEADS + h))],
        out_specs=pl.BlockSpec((t, MLA_V), lambda b, h, i: (b * nq + i, h)),
        out_shape=jax.ShapeDtypeStruct((batch * seq, MLA_HEADS * MLA_V), BF16),
        scratch_shapes=[pltpu.VMEM((t, 1), F32), pltpu.VMEM((t, 1), F32),
                        pltpu.VMEM((t, MLA_V), F32)],
        compiler_params=_params(3),
        name="mla_attn",
    )(qm, kv, kr, kv)


def _diff_attn_kernel(slope_ref, q_ref, k_ref, v_ref, lq1_ref, lk1_ref, lq2_ref, lk2_ref, sg_ref,
                      o_ref, m_sc, l_sc, acc_sc, *, t, lambda_init):
    h = pl.program_id(1)
    i = pl.program_id(2)
    d = DIFF_HEAD_DIM
    slope = slope_ref[h]
    q1 = q_ref[:, :d]
    q2 = q_ref[:, d:]
    m_sc[...] = jnp.full(m_sc.shape, NEG_INF, F32)
    l_sc[...] = jnp.zeros(l_sc.shape, F32)
    acc_sc[...] = jnp.zeros(acc_sc.shape, F32)
    nt = (((1,), (1,)), ((), ()))

    def scores(j):
        rows = pl.ds(pl.multiple_of(j * t, t), t)
        s1 = lax.dot_general(q1, k_ref[rows, :d], nt, preferred_element_type=F32)
        s2 = lax.dot_general(q2, k_ref[rows, d:], nt, preferred_element_type=F32)
        return jnp.concatenate([s1, s2], axis=0), v_ref[rows, :]

    def body(j, carry):
        s, v = scores(j)
        kpos = (lax.broadcasted_iota(jnp.int32, (1, t), 1) + (j - i) * t).astype(F32)
        _online_softmax_step(s + kpos * slope, v, m_sc, l_sc, acc_sc)
        return carry

    lax.fori_loop(0, i, body, 0)

    s, v = scores(i)
    qi = lax.broadcasted_iota(jnp.int32, (t, t), 0)
    ki = lax.broadcasted_iota(jnp.int32, (t, t), 1)
    bias = jnp.minimum(ki, 2 * qi - ki).astype(F32) * slope
    bias = jnp.where(_chunk_mask(t), bias, NEG_INF)
    s = s + jnp.concatenate([bias, bias], axis=0)
    _online_softmax_step(s, v, m_sc, l_sc, acc_sc)

    lam = (jnp.exp(jnp.sum(lq1_ref[...] * lk1_ref[...], axis=1, keepdims=True))
           - jnp.exp(jnp.sum(lq2_ref[...] * lk2_ref[...], axis=1, keepdims=True)) + lambda_init)
    o = acc_sc[...] / l_sc[...]
    o = o[:t] - lam * o[t:]
    o_ref[...] = (_rmsnorm_rows(o, sg_ref[...]) * (1.0 - lambda_init)).astype(o_ref.dtype)


def _diff_attn(proj, slopes, lq1, lk1, lq2, lk2, sub_g, batch, seq, t, lambda_init):
    nq = seq // t
    vec = lambda a: a.reshape(1, -1).astype(F32)
    small = lambda n: pl.BlockSpec((1, n), lambda b, h, i, s: (0, 0))
    grid_spec = pltpu.PrefetchScalarGridSpec(
        num_scalar_prefetch=1,
        grid=(batch, DIFF_HEADS, nq),
        in_specs=[pl.BlockSpec((t, DIFF_V), lambda b, h, i, s: (b * nq + i, COL_DQ // DIFF_V + h)),
                  pl.BlockSpec((seq, DIFF_V), lambda b, h, i, s: (b, COL_DK // DIFF_V + h)),
                  pl.BlockSpec((seq, DIFF_V), lambda b, h, i, s: (b, COL_DV // DIFF_V + h)),
                  small(DIFF_HEAD_DIM), small(DIFF_HEAD_DIM), small(DIFF_HEAD_DIM),
                  small(DIFF_HEAD_DIM), small(DIFF_V)],
        out_specs=pl.BlockSpec((t, DIFF_V), lambda b, h, i, s: (b * nq + i, h)),
        scratch_shapes=[pltpu.VMEM((2 * t, 1), F32), pltpu.VMEM((2 * t, 1), F32),
                        pltpu.VMEM((2 * t, DIFF_V), F32)],
    )
    return pl.pallas_call(
        functools.partial(_diff_attn_kernel, t=t, lambda_init=lambda_init),
        grid_spec=grid_spec,
        out_shape=jax.ShapeDtypeStruct((batch * seq, DIFF_HEADS * DIFF_V), BF16),
        compiler_params=_params(3),
        name="diff_attn",
    )(slopes, proj, proj, proj, vec(lq1), vec(lk1), vec(lq2), vec(lk2), vec(sub_g))


def _merge_kernel(ya_ref, yb_ref, wa_ref, wb_ref, ga_ref, gb_ref, ba_ref, bb_ref, o_ref):
    a = jnp.dot(ya_ref[...], wa_ref[...], preferred_element_type=F32)
    b = jnp.dot(yb_ref[...], wb_ref[...], preferred_element_type=F32)
    g0 = jax.nn.sigmoid(ga_ref[...].astype(F32) + ba_ref[...])
    g1 = jax.nn.sigmoid(gb_ref[...].astype(F32) + bb_ref[...])
    o_ref[...] = (g0 * a + g1 * b).astype(o_ref.dtype)


def _merge(y_mla, y_diff, w_mla, w_diff, proj, b_gate, tm, tn):
    m = y_mla.shape[0]
    d = D_MODEL
    g0_blk = COL_GATE // tn
    g1_blk = (COL_GATE + d) // tn
    nb = d // tn
    bg = b_gate.reshape(1, 2 * d).astype(F32)
    return pl.pallas_call(
        _merge_kernel,
        grid=(m // tm, nb),
        in_specs=[pl.BlockSpec((tm, d), lambda i, j: (i, 0)),
                  pl.BlockSpec((tm, d), lambda i, j: (i, 0)),
                  pl.BlockSpec((d, tn), lambda i, j: (0, j)),
                  pl.BlockSpec((d, tn), lambda i, j: (0, j)),
                  pl.BlockSpec((tm, tn), lambda i, j: (i, g0_blk + j)),
                  pl.BlockSpec((tm, tn), lambda i, j: (i, g1_blk + j)),
                  pl.BlockSpec((1, tn), lambda i, j: (0, j)),
                  pl.BlockSpec((1, tn), lambda i, j: (0, nb + j))],
        out_specs=pl.BlockSpec((tm, tn), lambda i, j: (i, j)),
        out_shape=jax.ShapeDtypeStruct((m, d), BF16),
        compiler_params=_params(2),
        name="gated_merge",
    )(y_mla, y_diff, w_mla, w_diff, proj, proj, bg, bg)


def _residual_matmul_kernel(a_ref, w_ref, x_ref, o_ref):
    o_ref[...] = x_ref[...] + jnp.dot(a_ref[...], w_ref[...], preferred_element_type=F32)


def _residual_matmul(a, w, x, tm, tn):
    m, k = a.shape
    n = w.shape[1]
    return pl.pallas_call(
        _residual_matmul_kernel,
        grid=(m // tm, n // tn),
        in_specs=[pl.BlockSpec((tm, k), lambda i, j: (i, 0)),
                  pl.BlockSpec((k, tn), lambda i, j: (0, j)),
                  pl.BlockSpec((tm, tn), lambda i, j: (i, j))],
        out_specs=pl.BlockSpec((tm, tn), lambda i, j: (i, j)),
        out_shape=jax.ShapeDtypeStruct((m, n), F32),
        compiler_params=_params(2),
        name="out_proj_residual",
    )(a, w, x)


def _ffn_up_kernel(x_ref, g_ref, wg_ref, wu_ref, o_ref, h_ref):
    @pl.when(pl.program_id(1) == 0)
    def _():
        h_ref[...] = _rmsnorm_rows(x_ref[...], g_ref[...]).astype(BF16)

    h = h_ref[...]
    gate = jnp.dot(h, wg_ref[...], preferred_element_type=F32)
    up = jnp.dot(h, wu_ref[...], preferred_element_type=F32)
    o_ref[...] = (gate * jax.nn.sigmoid(gate) * up).astype(o_ref.dtype)


def _ffn_up(x, g, wg, wu, tm, tn):
    m, d = x.shape
    n = wg.shape[1]
    return pl.pallas_call(
        _ffn_up_kernel,
        grid=(m // tm, n // tn),
        in_specs=[pl.BlockSpec((tm, d), lambda i, j: (i, 0)),
                  pl.BlockSpec((1, d), lambda i, j: (0, 0)),
                  pl.BlockSpec((d, tn), lambda i, j: (0, j)),
                  pl.BlockSpec((d, tn), lambda i, j: (0, j))],
        out_specs=pl.BlockSpec((tm, tn), lambda i, j: (i, j)),
        out_shape=jax.ShapeDtypeStruct((m, n), BF16),
        scratch_shapes=[pltpu.VMEM((tm, d), BF16)],
        compiler_params=_params(2),
        name="ffn_up",
    )(x, g.reshape(1, d), wg, wu)


def _ffn_down_kernel(a_ref, w_ref, x_ref, g_ref, o_ref, acc_ref, *, final_norm):
    kk = pl.program_id(1)

    @pl.when(kk == 0)
    def _():
        acc_ref[...] = x_ref[...]

    acc_ref[...] += jnp.dot(a_ref[...], w_ref[...], preferred_element_type=F32)

    @pl.when(kk == pl.num_programs(1) - 1)
    def _():
        y = acc_ref[...]
        o_ref[...] = _rmsnorm_rows(y, g_ref[...]) if final_norm else y


def _ffn_down(a, w, x, g, tm, tk, final_norm):
    m, kdim = a.shape
    d = w.shape[1]
    return pl.pallas_call(
        functools.partial(_ffn_down_kernel, final_norm=final_norm),
        grid=(m // tm, kdim // tk),
        in_specs=[pl.BlockSpec((tm, tk), lambda i, k: (i, k)),
                  pl.BlockSpec((tk, d), lambda i, k: (k, 0)),
                  pl.BlockSpec((tm, d), lambda i, k: (i, 0)),
                  pl.BlockSpec((1, d), lambda i, k: (0, 0))],
        out_specs=pl.BlockSpec((tm, d), lambda i, k: (i, 0)),
        out_shape=jax.ShapeDtypeStruct((m, d), F32),
        scratch_shapes=[pltpu.VMEM((tm, d), F32)],
        compiler_params=_params(2),
        name="ffn_down_final_norm",
    )(a, w, x, g.reshape(1, d))


def _rotate_half_cols(w):
    half = w.shape[-1] // 2
    return jnp.concatenate([-w[..., half:], w[..., :half]], axis=-1)


def _prep_w_in(w_in):
    d = D_MODEL
    sizes = (MLA_Q_RANK, MLA_KV_RANK, MLA_ROPE, d, d, d, 2 * d)
    offs = [0]
    for sz in sizes:
        offs.append(offs[-1] + sz)
    cq, ckv, kpe, dq, dk, dv, gate = [w_in[:, offs[n]:offs[n + 1]] for n in range(len(sizes))]
    zeros = lambda n: jnp.zeros((d, n), w_in.dtype)
    dq = dq * (LOG2E / math.sqrt(DIFF_HEAD_DIM))
    cols = [cq, kpe, _rotate_half_cols(kpe), zeros(COL_CKV - COL_KPE - 2 * MLA_ROPE),
            ckv, zeros(COL_GATE - COL_CKV - MLA_KV_RANK), gate, dq, dk, dv]
    return jnp.concatenate(cols, axis=1).astype(BF16)


def _prep_w_uq(w_uq):
    w = w_uq.reshape(MLA_Q_RANK, MLA_HEADS, MLA_QK)
    nope, rope = w[..., :MLA_NOPE], w[..., MLA_NOPE:]
    w = jnp.concatenate([nope, rope, _rotate_half_cols(rope)], axis=-1)
    return w.reshape(MLA_Q_RANK, MLA_HEADS * MLA_QPAD).astype(BF16)


def _prep_w_ukv(w_ukv):
    w = w_ukv.reshape(MLA_KV_RANK, MLA_HEADS, MLA_NOPE + MLA_V)
    k_nope = w[..., :MLA_NOPE].reshape(MLA_KV_RANK, MLA_HEADS * MLA_NOPE)
    v = w[..., MLA_NOPE:].reshape(MLA_KV_RANK, MLA_HEADS * MLA_V)
    return jnp.concatenate([k_nope, v], axis=1).astype(BF16)


def _rope_tables(seq):
    inv = ROPE_THETA ** (-jnp.arange(0, MLA_ROPE, 2, dtype=F32) / MLA_ROPE)
    ang = jnp.arange(seq, dtype=F32)[:, None] * inv[None, :]
    cos, sin = jnp.cos(ang), jnp.sin(ang)
    zeros = jnp.zeros((seq, LANE - MLA_ROPE), F32)
    return (jnp.concatenate([cos, cos, zeros], axis=1), jnp.concatenate([sin, sin, zeros], axis=1))


def _tile(n, pref):
    t = pref
    while n % t:
        t //= 2
    return t


def kernel(x, attn_norm_g, w_in, b_gate, q_norm_g, w_uq, kv_norm_g, w_ukv, lambda_q1, lambda_k1,
           lambda_q2, lambda_k2, diff_norm_g, w_mla_proj, w_diff_proj, w_out, ffn_norm_g,
           w_ffn_gate, w_ffn_up, w_ffn_down, final_norm_g):
    batch, seq, d = x.shape
    depth = w_in.shape[0]
    assert d == D_MODEL and seq % 512 == 0
    tokens = batch * seq
    xt = x.reshape(tokens, d)
    c1, c2 = _rope_tables(seq)
    slopes = jnp.exp2(-8.0 * jnp.arange(1, DIFF_HEADS + 1, dtype=F32) / DIFF_HEADS) * LOG2E
    tm = _tile(seq, 1024)
    t_attn = 512
    d_ff = w_ffn_gate.shape[-1]

    for layer in range(depth):
        lambda_init = 0.8 - 0.6 * math.exp(-0.3 * layer)
        proj = _norm_matmul(xt, d, 0, attn_norm_g[layer], _prep_w_in(w_in[layer]), tm, 1024)
        qm = _q_proj(proj, q_norm_g[layer], _prep_w_uq(w_uq[layer]), c1, c2, seq, tm, 1024)
        kv = _norm_matmul(proj, MLA_KV_RANK, COL_CKV // MLA_KV_RANK, kv_norm_g[layer],
                          _prep_w_ukv(w_ukv[layer]), tm, 1024)
        kr = _k_rope(proj, c1, c2, seq, tm)
        y_mla = _mla_attn(qm, kv, kr, batch, seq, t_attn)
        y_diff = _diff_attn(proj, slopes, lambda_q1[layer], lambda_k1[layer], lambda_q2[layer],
                            lambda_k2[layer], diff_norm_g[layer], batch, seq, t_attn, lambda_init)
        merged = _merge(y_mla, y_diff, w_mla_proj[layer].astype(BF16),
                        w_diff_proj[layer].astype(BF16), proj, b_gate[layer], tm, 512)
        xt = _residual_matmul(merged, w_out[layer].astype(BF16), xt, tm, 1024)
        a = _ffn_up(xt, ffn_norm_g[layer], w_ffn_gate[layer].astype(BF16),
                    w_ffn_up[layer].astype(BF16), tm, 512)
        xt = _ffn_down(a, w_ffn_down[layer].astype(BF16), xt, final_norm_g, 512, d_ff // 4,
                       final_norm=layer == depth - 1)
    return xt.reshape(batch, seq, d)
```

```python
import functools
import math

import jax
import jax.numpy as jnp
from jax import lax
from jax.experimental import pallas as pl
from jax.experimental.pallas import tpu as pltpu

D_MODEL = 2048
CHUNK = 64
EPS = 1e-6
NEG_INF = -1e30
LOG2E = math.log2(math.e)

MLA_HEADS = 16
MLA_Q_RANK = 768
MLA_KV_RANK = 512
MLA_NOPE = 128
MLA_ROPE = 64
MLA_V = 128
MLA_QK = MLA_NOPE + MLA_ROPE
ROPE_THETA = 10000.0

DIFF_HEADS = 8
DIFF_HEAD_DIM = 128
DIFF_V = 2 * DIFF_HEAD_DIM

LANE = 128
MLA_QPAD = 2 * LANE

COL_CQ = 0
COL_KPE = 768
COL_CKV = 1024
COL_GATE = 2048
COL_DQ = COL_GATE + 2 * D_MODEL
COL_DK = COL_DQ + D_MODEL
COL_DV = COL_DK + D_MODEL
PROJ_COLS = COL_DV + D_MODEL

VMEM_LIMIT_BYTES = 56 * 1024 * 1024

BF16 = jnp.bfloat16
F32 = jnp.float32


def _params(n_axes):
    return pltpu.CompilerParams(dimension_semantics=("arbitrary",) * n_axes,
                                vmem_limit_bytes=VMEM_LIMIT_BYTES)


def _rmsnorm_rows(x, g):
    ms = jnp.mean(x * x, axis=-1, keepdims=True)
    return x * lax.rsqrt(ms + EPS) * g


def _norm_matmul_kernel(x_ref, g_ref, w_ref, o_ref, h_ref):
    @pl.when(pl.program_id(1) == 0)
    def _():
        h_ref[...] = _rmsnorm_rows(x_ref[...].astype(F32), g_ref[...]).astype(BF16)

    o_ref[...] = jnp.dot(h_ref[...], w_ref[...], preferred_element_type=F32).astype(o_ref.dtype)


def _norm_matmul(x, x_cols, x_col_block, g, w, tm, tn):
    m = x.shape[0]
    n = w.shape[1]
    return pl.pallas_call(
        _norm_matmul_kernel,
        grid=(m // tm, n // tn),
        in_specs=[pl.BlockSpec((tm, x_cols), lambda i, j: (i, x_col_block)),
                  pl.BlockSpec((1, x_cols), lambda i, j: (0, 0)),
                  pl.BlockSpec((x_cols, tn), lambda i, j: (0, j))],
        out_specs=pl.BlockSpec((tm, tn), lambda i, j: (i, j)),
        out_shape=jax.ShapeDtypeStruct((m, n), BF16),
        scratch_shapes=[pltpu.VMEM((tm, x_cols), BF16)],
        compiler_params=_params(2),
        name="norm_matmul",
    )(x, g.reshape(1, x_cols), w)


def _rope_block(blk, c1, c2):
    return blk * c1 + pltpu.roll(blk, LANE // 2, 1) * c2


def _q_proj_kernel(x_ref, g_ref, w_ref, c1_ref, c2_ref, o_ref, h_ref, *, heads_per_tile, qscale):
    @pl.when(pl.program_id(1) == 0)
    def _():
        h_ref[...] = _rmsnorm_rows(x_ref[...].astype(F32), g_ref[...]).astype(BF16)

    acc = jnp.dot(h_ref[...], w_ref[...], preferred_element_type=F32)
    c1 = c1_ref[...]
    c2 = c2_ref[...]
    for hh in range(heads_per_tile):
        lo = hh * MLA_QPAD
        o_ref[:, lo:lo + LANE] = (acc[:, lo:lo + LANE] * qscale).astype(o_ref.dtype)
        roped = _rope_block(acc[:, lo + LANE:lo + MLA_QPAD], c1, c2)
        o_ref[:, lo + LANE:lo + MLA_QPAD] = (roped * qscale).astype(o_ref.dtype)


def _q_proj(proj, g, w, c1, c2, seq, tm, tn):
    m = proj.shape[0]
    n = w.shape[1]
    s_tiles = seq // tm
    return pl.pallas_call(
        functools.partial(_q_proj_kernel, heads_per_tile=tn // MLA_QPAD,
                          qscale=LOG2E / math.sqrt(MLA_QK)),
        grid=(m // tm, n // tn),
        in_specs=[pl.BlockSpec((tm, MLA_Q_RANK), lambda i, j: (i, COL_CQ // MLA_Q_RANK)),
                  pl.BlockSpec((1, MLA_Q_RANK), lambda i, j: (0, 0)),
                  pl.BlockSpec((MLA_Q_RANK, tn), lambda i, j: (0, j)),
                  pl.BlockSpec((tm, LANE), lambda i, j: (i % s_tiles, 0)),
                  pl.BlockSpec((tm, LANE), lambda i, j: (i % s_tiles, 0))],
        out_specs=pl.BlockSpec((tm, tn), lambda i, j: (i, j)),
        out_shape=jax.ShapeDtypeStruct((m, n), BF16),
        scratch_shapes=[pltpu.VMEM((tm, MLA_Q_RANK), BF16)],
        compiler_params=_params(2),
        name="mla_q_proj",
    )(proj, g.reshape(1, MLA_Q_RANK), w, c1, c2)


def _k_rope_kernel(x_ref, c1_ref, c2_ref, o_ref):
    o_ref[...] = _rope_block(x_ref[...].astype(F32), c1_ref[...], c2_ref[...]).astype(o_ref.dtype)


def _k_rope(proj, c1, c2, seq, tm):
    m = proj.shape[0]
    s_tiles = seq // tm
    return pl.pallas_call(
        _k_rope_kernel,
        grid=(m // tm,),
        in_specs=[pl.BlockSpec((tm, LANE), lambda i: (i, COL_KPE // LANE)),
                  pl.BlockSpec((tm, LANE), lambda i: (i % s_tiles, 0)),
                  pl.BlockSpec((tm, LANE), lambda i: (i % s_tiles, 0))],
        out_specs=pl.BlockSpec((tm, LANE), lambda i: (i, 0)),
        out_shape=jax.ShapeDtypeStruct((m, LANE), BF16),
        compiler_params=_params(1),
        name="mla_k_rope",
    )(proj, c1, c2)


ONES_ROWS = 16
NT_DIMS = (((1,), (1,)), ((), ()))


def _chunk_mask_t(t):
    shift = int(math.log2(CHUNK))
    kc = lax.shift_right_logical(lax.broadcasted_iota(jnp.int32, (t, t), 0), shift)
    qc = lax.shift_right_logical(lax.broadcasted_iota(jnp.int32, (t, t), 1), shift)
    return kc <= qc


def _running_max(col_max, m_sc):
    m_prev = m_sc[...]
    m_new = jnp.maximum(m_prev, col_max)
    return m_new, jnp.exp2(m_prev - m_new)


def _accumulate(s_t, m_sub, alpha, vt, acc_sc):
    p_t = jnp.exp2(s_t - m_sub).astype(BF16)
    v_aug = jnp.concatenate([vt, jnp.ones((ONES_ROWS, vt.shape[1]), BF16)], axis=0)
    acc_sc[...] = alpha * acc_sc[...] + jnp.dot(v_aug, p_t, preferred_element_type=F32)


def _pipelined_blocks(n_full, qk_stage, pv_stage, final_stage, bufs):
    buf0, buf1 = bufs
    qk_stage(0, *buf0)

    def pair(p, carry):
        j = 2 * p
        qk_stage(j + 1, *buf1)
        pv_stage(j, *buf0)
        qk_stage(j + 2, *buf0)
        pv_stage(j + 1, *buf1)
        return carry

    lax.fori_loop(0, lax.div(n_full, 2), pair, 0)

    def odd_tail():
        qk_stage(n_full, *buf1)
        pv_stage(n_full - 1, *buf0)
        final_stage(*buf1)

    lax.cond(lax.rem(n_full, 2) == 1, odd_tail, lambda: final_stage(*buf0))


def _mla_attn_kernel(q_ref, kn_ref, kr_ref, vt_ref, o_ref, m_sc, acc_sc, s0_sc, s1_sc, cm0_sc,
                     cm1_sc, *, t):
    i = pl.program_id(2)
    q = q_ref[...]
    m_sc[...] = jnp.full(m_sc.shape, NEG_INF, F32)
    acc_sc[...] = jnp.zeros(acc_sc.shape, F32)

    def key_rows(j):
        return pl.ds(pl.multiple_of(j * t, t), t)

    def qk_stage(j, s_sc, cm_sc):
        rows = key_rows(j)
        k = jnp.concatenate([kn_ref[rows, :], kr_ref[rows, :]], axis=1)
        s_t = lax.dot_general(k, q, NT_DIMS, preferred_element_type=F32)
        s_sc[...] = s_t
        cm_sc[...] = jnp.max(s_t, axis=0, keepdims=True)

    def pv_stage(j, s_sc, cm_sc):
        m_new, alpha = _running_max(cm_sc[...], m_sc)
        _accumulate(s_sc[...], m_new, alpha, vt_ref[:, key_rows(j)], acc_sc)
        m_sc[...] = m_new

    def final_stage(s_sc, cm_sc):
        s_t = jnp.where(_chunk_mask_t(t), s_sc[...], NEG_INF)
        m_new, alpha = _running_max(jnp.max(s_t, axis=0, keepdims=True), m_sc)
        _accumulate(s_t, m_new, alpha, vt_ref[:, key_rows(i)], acc_sc)
        acc = acc_sc[...]
        o_t = acc[:MLA_V] / acc[MLA_V:MLA_V + 1]
        o_ref[...] = o_t.T.astype(o_ref.dtype)

    _pipelined_blocks(i, qk_stage, pv_stage, final_stage, ((s0_sc, cm0_sc), (s1_sc, cm1_sc)))


def _mla_attn(qm, kv, kr, vt, batch, seq, t):
    nq = seq // t
    return pl.pallas_call(
        functools.partial(_mla_attn_kernel, t=t),
        grid=(batch, MLA_HEADS, nq),
        in_specs=[pl.BlockSpec((t, MLA_QPAD), lambda b, h, i: (b * nq + i, h)),
                  pl.BlockSpec((seq, LANE), lambda b, h, i: (b, h)),
                  pl.BlockSpec((seq, LANE), lambda b, h, i: (b, 0)),
                  pl.BlockSpec((MLA_V, seq), lambda b, h, i: (h, b))],
        out_specs=pl.BlockSpec((t, MLA_V), lambda b, h, i: (b * nq + i, h)),
        out_shape=jax.ShapeDtypeStruct((batch * seq, MLA_HEADS * MLA_V), BF16),
        scratch_shapes=[pltpu.VMEM((1, t), F32),
                        pltpu.VMEM((MLA_V + ONES_ROWS, t), F32),
                        pltpu.VMEM((t, t), F32), pltpu.VMEM((t, t), F32),
                        pltpu.VMEM((1, t), F32), pltpu.VMEM((1, t), F32)],
        compiler_params=_params(3),
        name="mla_attn",
    )(qm, kv, kr, vt)


def _diff_attn_kernel(slope_ref, q_ref, k_ref, vt_ref, lq1_ref, lk1_ref, lq2_ref, lk2_ref, sg_ref,
                      o_ref, m_sc, acc_sc, bias_sc, s0_sc, s1_sc, cm0_sc, cm1_sc, *, t,
                      lambda_init):
    h = pl.program_id(1)
    i = pl.program_id(2)
    d = DIFF_HEAD_DIM
    slope = slope_ref[h]
    q1 = q_ref[:, :d]
    q2 = q_ref[:, d:]
    m_sc[...] = jnp.full(m_sc.shape, NEG_INF, F32)
    acc_sc[...] = jnp.zeros(acc_sc.shape, F32)
    key_idx = lax.broadcasted_iota(jnp.int32, (t, t), 0)
    bias_sc[...] = key_idx.astype(F32) * slope

    def key_rows(j):
        return pl.ds(pl.multiple_of(j * t, t), t)

    def qk_stage(j, s_sc, cm_sc):
        rows = key_rows(j)
        bias = bias_sc[...]
        s1 = lax.dot_general(k_ref[rows, :d], q1, NT_DIMS, preferred_element_type=F32) + bias
        s2 = lax.dot_general(k_ref[rows, d:], q2, NT_DIMS, preferred_element_type=F32) + bias
        s_t = jnp.concatenate([s1, s2], axis=1)
        s_sc[...] = s_t
        cm_sc[...] = jnp.max(s_t, axis=0, keepdims=True)

    def pv_stage(j, s_sc, cm_sc):
        shift = ((j - i) * t).astype(F32) * slope
        m_new, alpha = _running_max(cm_sc[...] + shift, m_sc)
        _accumulate(s_sc[...], m_new - shift, alpha, vt_ref[:, key_rows(j)], acc_sc)
        m_sc[...] = m_new

    def final_stage(s_sc, cm_sc):
        qry_idx = lax.broadcasted_iota(jnp.int32, (t, t), 1)
        fix = jnp.maximum(key_idx - qry_idx, 0).astype(F32) * (-2.0 * slope)
        fix = jnp.where(_chunk_mask_t(t), fix, NEG_INF)
        s_t = s_sc[...] + jnp.concatenate([fix, fix], axis=1)
        m_new, alpha = _running_max(jnp.max(s_t, axis=0, keepdims=True), m_sc)
        _accumulate(s_t, m_new, alpha, vt_ref[:, key_rows(i)], acc_sc)
        lam = (jnp.exp(jnp.sum(lq1_ref[...] * lk1_ref[...], axis=1, keepdims=True))
               - jnp.exp(jnp.sum(lq2_ref[...] * lk2_ref[...], axis=1, keepdims=True))
               + lambda_init)
        acc = acc_sc[...]
        o_t = acc[:DIFF_V] / acc[DIFF_V:DIFF_V + 1]
        o = (o_t[:, :t] - lam * o_t[:, t:]).T
        o_ref[...] = (_rmsnorm_rows(o, sg_ref[...]) * (1.0 - lambda_init)).astype(o_ref.dtype)

    _pipelined_blocks(i, qk_stage, pv_stage, final_stage, ((s0_sc, cm0_sc), (s1_sc, cm1_sc)))


def _diff_attn(proj, dvt, slopes, lq1, lk1, lq2, lk2, sub_g, batch, seq, t, lambda_init):
    nq = seq // t
    vec = lambda a: a.reshape(1, -1).astype(F32)
    small = lambda n: pl.BlockSpec((1, n), lambda b, h, i, s: (0, 0))
    grid_spec = pltpu.PrefetchScalarGridSpec(
        num_scalar_prefetch=1,
        grid=(batch, DIFF_HEADS, nq),
        in_specs=[pl.BlockSpec((t, DIFF_V), lambda b, h, i, s: (b * nq + i, COL_DQ // DIFF_V + h)),
                  pl.BlockSpec((seq, DIFF_V), lambda b, h, i, s: (b, COL_DK // DIFF_V + h)),
                  pl.BlockSpec((DIFF_V, seq), lambda b, h, i, s: (h, b)),
                  small(DIFF_HEAD_DIM), small(DIFF_HEAD_DIM), small(DIFF_HEAD_DIM),
                  small(DIFF_HEAD_DIM), small(DIFF_V)],
        out_specs=pl.BlockSpec((t, DIFF_V), lambda b, h, i, s: (b * nq + i, h)),
        scratch_shapes=[pltpu.VMEM((1, 2 * t), F32),
                        pltpu.VMEM((DIFF_V + ONES_ROWS, 2 * t), F32),
                        pltpu.VMEM((t, t), F32),
                        pltpu.VMEM((t, 2 * t), F32), pltpu.VMEM((t, 2 * t), F32),
                        pltpu.VMEM((1, 2 * t), F32), pltpu.VMEM((1, 2 * t), F32)],
    )
    return pl.pallas_call(
        functools.partial(_diff_attn_kernel, t=t, lambda_init=lambda_init),
        grid_spec=grid_spec,
        out_shape=jax.ShapeDtypeStruct((batch * seq, DIFF_HEADS * DIFF_V), BF16),
        compiler_params=_params(3),
        name="diff_attn",
    )(slopes, proj, proj, dvt, vec(lq1), vec(lk1), vec(lq2), vec(lk2), vec(sub_g))


def _merge_kernel(ya_ref, yb_ref, wa_ref, wb_ref, ga_ref, gb_ref, ba_ref, bb_ref, o_ref):
    a = jnp.dot(ya_ref[...], wa_ref[...], preferred_element_type=F32)
    b = jnp.dot(yb_ref[...], wb_ref[...], preferred_element_type=F32)
    g0 = jax.nn.sigmoid(ga_ref[...].astype(F32) + ba_ref[...])
    g1 = jax.nn.sigmoid(gb_ref[...].astype(F32) + bb_ref[...])
    o_ref[...] = (g0 * a + g1 * b).astype(o_ref.dtype)


def _merge(y_mla, y_diff, w_mla, w_diff, proj, b_gate, tm, tn):
    m = y_mla.shape[0]
    d = D_MODEL
    g0_blk = COL_GATE // tn
    g1_blk = (COL_GATE + d) // tn
    nb = d // tn
    bg = b_gate.reshape(1, 2 * d).astype(F32)
    return pl.pallas_call(
        _merge_kernel,
        grid=(m // tm, nb),
        in_specs=[pl.BlockSpec((tm, d), lambda i, j: (i, 0)),
                  pl.BlockSpec((tm, d), lambda i, j: (i, 0)),
                  pl.BlockSpec((d, tn), lambda i, j: (0, j)),
                  pl.BlockSpec((d, tn), lambda i, j: (0, j)),
                  pl.BlockSpec((tm, tn), lambda i, j: (i, g0_blk + j)),
                  pl.BlockSpec((tm, tn), lambda i, j: (i, g1_blk + j)),
                  pl.BlockSpec((1, tn), lambda i, j: (0, j)),
                  pl.BlockSpec((1, tn), lambda i, j: (0, nb + j))],
        out_specs=pl.BlockSpec((tm, tn), lambda i, j: (i, j)),
        out_shape=jax.ShapeDtypeStruct((m, d), BF16),
        compiler_params=_params(2),
        name="gated_merge",
    )(y_mla, y_diff, w_mla, w_diff, proj, proj, bg, bg)


def _residual_matmul_kernel(a_ref, w_ref, x_ref, o_ref):
    o_ref[...] = x_ref[...] + jnp.dot(a_ref[...], w_ref[...], preferred_element_type=F32)


def _residual_matmul(a, w, x, tm, tn):
    m, k = a.shape
    n = w.shape[1]
    return pl.pallas_call(
        _residual_matmul_kernel,
        grid=(m // tm, n // tn),
        in_specs=[pl.BlockSpec((tm, k), lambda i, j: (i, 0)),
                  pl.BlockSpec((k, tn), lambda i, j: (0, j)),
                  pl.BlockSpec((tm, tn), lambda i, j: (i, j))],
        out_specs=pl.BlockSpec((tm, tn), lambda i, j: (i, j)),
        out_shape=jax.ShapeDtypeStruct((m, n), F32),
        compiler_params=_params(2),
        name="out_proj_residual",
    )(a, w, x)


def _ffn_up_kernel(x_ref, g_ref, wg_ref, wu_ref, o_ref, h_ref):
    @pl.when(pl.program_id(1) == 0)
    def _():
        h_ref[...] = _rmsnorm_rows(x_ref[...], g_ref[...]).astype(BF16)

    h = h_ref[...]
    gate = jnp.dot(h, wg_ref[...], preferred_element_type=F32)
    up = jnp.dot(h, wu_ref[...], preferred_element_type=F32)
    o_ref[...] = (gate * jax.nn.sigmoid(gate) * up).astype(o_ref.dtype)


def _ffn_up(x, g, wg, wu, tm, tn):
    m, d = x.shape
    n = wg.shape[1]
    return pl.pallas_call(
        _ffn_up_kernel,
        grid=(m // tm, n // tn),
        in_specs=[pl.BlockSpec((tm, d), lambda i, j: (i, 0)),
                  pl.BlockSpec((1, d), lambda i, j: (0, 0)),
                  pl.BlockSpec((d, tn), lambda i, j: (0, j)),
                  pl.BlockSpec((d, tn), lambda i, j: (0, j))],
        out_specs=pl.BlockSpec((tm, tn), lambda i, j: (i, j)),
        out_shape=jax.ShapeDtypeStruct((m, n), BF16),
        scratch_shapes=[pltpu.VMEM((tm, d), BF16)],
        compiler_params=_params(2),
        name="ffn_up",
    )(x, g.reshape(1, d), wg, wu)


def _ffn_down_kernel(a_ref, w_ref, x_ref, g_ref, o_ref, acc_ref, *, final_norm):
    kk = pl.program_id(1)

    @pl.when(kk == 0)
    def _():
        acc_ref[...] = x_ref[...]

    acc_ref[...] += jnp.dot(a_ref[...], w_ref[...], preferred_element_type=F32)

    @pl.when(kk == pl.num_programs(1) - 1)
    def _():
        y = acc_ref[...]
        o_ref[...] = _rmsnorm_rows(y, g_ref[...]) if final_norm else y


def _ffn_down(a, w, x, g, tm, tk, final_norm):
    m, kdim = a.shape
    d = w.shape[1]
    return pl.pallas_call(
        functools.partial(_ffn_down_kernel, final_norm=final_norm),
        grid=(m // tm, kdim // tk),
        in_specs=[pl.BlockSpec((tm, tk), lambda i, k: (i, k)),
                  pl.BlockSpec((tk, d), lambda i, k: (k, 0)),
                  pl.BlockSpec((tm, d), lambda i, k: (i, 0)),
                  pl.BlockSpec((1, d), lambda i, k: (0, 0))],
        out_specs=pl.BlockSpec((tm, d), lambda i, k: (i, 0)),
        out_shape=jax.ShapeDtypeStruct((m, d), F32),
        scratch_shapes=[pltpu.VMEM((tm, d), F32)],
        compiler_params=_params(2),
        name="ffn_down_final_norm",
    )(a, w, x, g.reshape(1, d))


def _rotate_half_cols(w):
    half = w.shape[-1] // 2
    return jnp.concatenate([-w[..., half:], w[..., :half]], axis=-1)


def _prep_w_in(w_in):
    d = D_MODEL
    sizes = (MLA_Q_RANK, MLA_KV_RANK, MLA_ROPE, d, d, d, 2 * d)
    offs = [0]
    for sz in sizes:
        offs.append(offs[-1] + sz)
    cq, ckv, kpe, dq, dk, dv, gate = [w_in[:, offs[n]:offs[n + 1]] for n in range(len(sizes))]
    zeros = lambda n: jnp.zeros((d, n), w_in.dtype)
    dq = dq * (LOG2E / math.sqrt(DIFF_HEAD_DIM))
    cols = [cq, kpe, _rotate_half_cols(kpe), zeros(COL_CKV - COL_KPE - 2 * MLA_ROPE),
            ckv, zeros(COL_GATE - COL_CKV - MLA_KV_RANK), gate, dq, dk, dv]
    return jnp.concatenate(cols, axis=1).astype(BF16)


def _prep_w_uq(w_uq):
    w = w_uq.reshape(MLA_Q_RANK, MLA_HEADS, MLA_QK)
    nope, rope = w[..., :MLA_NOPE], w[..., MLA_NOPE:]
    w = jnp.concatenate([nope, rope, _rotate_half_cols(rope)], axis=-1)
    return w.reshape(MLA_Q_RANK, MLA_HEADS * MLA_QPAD).astype(BF16)


def _prep_w_ukv(w_ukv):
    w = w_ukv.reshape(MLA_KV_RANK, MLA_HEADS, MLA_NOPE + MLA_V)
    k_nope = w[..., :MLA_NOPE].reshape(MLA_KV_RANK, MLA_HEADS * MLA_NOPE)
    v = w[..., MLA_NOPE:].reshape(MLA_KV_RANK, MLA_HEADS * MLA_V)
    return jnp.concatenate([k_nope, v], axis=1).astype(BF16)


def _rope_tables(seq):
    inv = ROPE_THETA ** (-jnp.arange(0, MLA_ROPE, 2, dtype=F32) / MLA_ROPE)
    ang = jnp.arange(seq, dtype=F32)[:, None] * inv[None, :]
    cos, sin = jnp.cos(ang), jnp.sin(ang)
    zeros = jnp.zeros((seq, LANE - MLA_ROPE), F32)
    return (jnp.concatenate([cos, cos, zeros], axis=1), jnp.concatenate([sin, sin, zeros], axis=1))


def _tile(n, pref):
    t = pref
    while n % t:
        t //= 2
    return t


def kernel(x, attn_norm_g, w_in, b_gate, q_norm_g, w_uq, kv_norm_g, w_ukv, lambda_q1, lambda_k1,
           lambda_q2, lambda_k2, diff_norm_g, w_mla_proj, w_diff_proj, w_out, ffn_norm_g,
           w_ffn_gate, w_ffn_up, w_ffn_down, final_norm_g):
    batch, seq, d = x.shape
    depth = w_in.shape[0]
    assert d == D_MODEL and seq % 512 == 0
    tokens = batch * seq
    xt = x.reshape(tokens, d)
    c1, c2 = _rope_tables(seq)
    slopes = jnp.exp2(-8.0 * jnp.arange(1, DIFF_HEADS + 1, dtype=F32) / DIFF_HEADS) * LOG2E
    tm = _tile(seq, 1024)
    t_attn = 512
    d_ff = w_ffn_gate.shape[-1]

    for layer in range(depth):
        lambda_init = 0.8 - 0.6 * math.exp(-0.3 * layer)
        proj = _norm_matmul(xt, d, 0, attn_norm_g[layer], _prep_w_in(w_in[layer]), tm, 1024)
        qm = _q_proj(proj, q_norm_g[layer], _prep_w_uq(w_uq[layer]), c1, c2, seq, tm, 1024)
        kv = _norm_matmul(proj, MLA_KV_RANK, COL_CKV // MLA_KV_RANK, kv_norm_g[layer],
                          _prep_w_ukv(w_ukv[layer]), tm, 1024)
        kr = _k_rope(proj, c1, c2, seq, tm)
        vt = kv[:, MLA_HEADS * MLA_NOPE:].T
        dvt = proj[:, COL_DV:].T
        y_mla = _mla_attn(qm, kv, kr, vt, batch, seq, t_attn)
        y_diff = _diff_attn(proj, dvt, slopes, lambda_q1[layer], lambda_k1[layer], lambda_q2[layer],
                            lambda_k2[layer], diff_norm_g[layer], batch, seq, t_attn, lambda_init)
        merged = _merge(y_mla, y_diff, w_mla_proj[layer].astype(BF16),
                        w_diff_proj[layer].astype(BF16), proj, b_gate[layer], tm, 512)
        xt = _residual_matmul(merged, w_out[layer].astype(BF16), xt, tm, 1024)
        a = _ffn_up(xt, ffn_norm_g[layer], w_ffn_gate[layer].astype(BF16),
                    w_ffn_up[layer].astype(BF16), tm, 512)
        xt = _ffn_down(a, w_ffn_down[layer].astype(BF16), xt, final_norm_g, 512, d_ff // 4,
                       final_norm=layer == depth - 1)
    return xt.reshape(batch, seq, d)
```

```python
import functools
import math

import jax
import jax.numpy as jnp
from jax import lax
from jax.experimental import pallas as pl
from jax.experimental.pallas import tpu as pltpu

D_MODEL = 2048
CHUNK = 64
EPS = 1e-6
NEG_INF = -1e30
LOG2E = math.log2(math.e)

MLA_HEADS = 16
MLA_Q_RANK = 768
MLA_KV_RANK = 512
MLA_NOPE = 128
MLA_ROPE = 64
MLA_V = 128
MLA_QK = MLA_NOPE + MLA_ROPE
ROPE_THETA = 10000.0

DIFF_HEADS = 8
DIFF_HEAD_DIM = 128
DIFF_V = 2 * DIFF_HEAD_DIM

LANE = 128
MLA_QPAD = 2 * LANE

COL_CQ = 0
COL_KPE = 768
COL_CKV = 1024
COL_GATE = 2048
COL_DQ = COL_GATE + 2 * D_MODEL
COL_DK = COL_DQ + D_MODEL
PROJ_COLS = COL_DK + D_MODEL

VMEM_LIMIT_BYTES = 56 * 1024 * 1024

BF16 = jnp.bfloat16
F32 = jnp.float32
NT_DIMS = (((1,), (1,)), ((), ()))


def _params(n_axes):
    return pltpu.CompilerParams(dimension_semantics=("arbitrary",) * n_axes,
                                vmem_limit_bytes=VMEM_LIMIT_BYTES)


def _rmsnorm_rows(x, g):
    ms = jnp.mean(x * x, axis=-1, keepdims=True)
    return x * lax.rsqrt(ms + EPS) * g


def _norm_matmul_kernel(x_ref, g_ref, w_ref, o_ref, h_ref):
    @pl.when(pl.program_id(1) == 0)
    def _():
        h_ref[...] = _rmsnorm_rows(x_ref[...].astype(F32), g_ref[...]).astype(BF16)

    o_ref[...] = jnp.dot(h_ref[...], w_ref[...], preferred_element_type=F32).astype(o_ref.dtype)


def _norm_matmul(x, x_cols, x_col_block, g, w, tm, tn):
    m = x.shape[0]
    n = w.shape[1]
    return pl.pallas_call(
        _norm_matmul_kernel,
        grid=(m // tm, n // tn),
        in_specs=[pl.BlockSpec((tm, x_cols), lambda i, j: (i, x_col_block)),
                  pl.BlockSpec((1, x_cols), lambda i, j: (0, 0)),
                  pl.BlockSpec((x_cols, tn), lambda i, j: (0, j))],
        out_specs=pl.BlockSpec((tm, tn), lambda i, j: (i, j)),
        out_shape=jax.ShapeDtypeStruct((m, n), BF16),
        scratch_shapes=[pltpu.VMEM((tm, x_cols), BF16)],
        compiler_params=_params(2),
        name="norm_matmul",
    )(x, g.reshape(1, x_cols), w)


def _norm_matmul_nt_kernel(x_ref, g_ref, wt_ref, o_ref, h_ref):
    @pl.when(pl.program_id(1) == 0)
    def _():
        h_ref[...] = _rmsnorm_rows(x_ref[...].astype(F32), g_ref[...]).astype(BF16)

    o_ref[...] = lax.dot_general(wt_ref[...], h_ref[...], NT_DIMS,
                                 preferred_element_type=F32).astype(o_ref.dtype)


def _norm_matmul_nt(x, x_cols, x_col_block, g, wt, tm, tn):
    m = x.shape[0]
    n = wt.shape[0]
    return pl.pallas_call(
        _norm_matmul_nt_kernel,
        grid=(m // tm, n // tn),
        in_specs=[pl.BlockSpec((tm, x_cols), lambda i, j: (i, x_col_block)),
                  pl.BlockSpec((1, x_cols), lambda i, j: (0, 0)),
                  pl.BlockSpec((tn, x_cols), lambda i, j: (j, 0))],
        out_specs=pl.BlockSpec((tn, tm), lambda i, j: (j, i)),
        out_shape=jax.ShapeDtypeStruct((n, m), BF16),
        scratch_shapes=[pltpu.VMEM((tm, x_cols), BF16)],
        compiler_params=_params(2),
        name="norm_matmul_nt",
    )(x, g.reshape(1, x_cols), wt)


def _rope_block(blk, c1, c2):
    return blk * c1 + pltpu.roll(blk, LANE // 2, 1) * c2


def _q_proj_kernel(x_ref, g_ref, w_ref, c1_ref, c2_ref, o_ref, h_ref, *, heads_per_tile, qscale):
    @pl.when(pl.program_id(1) == 0)
    def _():
        h_ref[...] = _rmsnorm_rows(x_ref[...].astype(F32), g_ref[...]).astype(BF16)

    acc = jnp.dot(h_ref[...], w_ref[...], preferred_element_type=F32)
    c1 = c1_ref[...]
    c2 = c2_ref[...]
    for hh in range(heads_per_tile):
        lo = hh * MLA_QPAD
        o_ref[:, lo:lo + LANE] = (acc[:, lo:lo + LANE] * qscale).astype(o_ref.dtype)
        roped = _rope_block(acc[:, lo + LANE:lo + MLA_QPAD], c1, c2)
        o_ref[:, lo + LANE:lo + MLA_QPAD] = (roped * qscale).astype(o_ref.dtype)


def _q_proj(proj, g, w, c1, c2, seq, tm, tn):
    m = proj.shape[0]
    n = w.shape[1]
    s_tiles = seq // tm
    return pl.pallas_call(
        functools.partial(_q_proj_kernel, heads_per_tile=tn // MLA_QPAD,
                          qscale=LOG2E / math.sqrt(MLA_QK)),
        grid=(m // tm, n // tn),
        in_specs=[pl.BlockSpec((tm, MLA_Q_RANK), lambda i, j: (i, COL_CQ // MLA_Q_RANK)),
                  pl.BlockSpec((1, MLA_Q_RANK), lambda i, j: (0, 0)),
                  pl.BlockSpec((MLA_Q_RANK, tn), lambda i, j: (0, j)),
                  pl.BlockSpec((tm, LANE), lambda i, j: (i % s_tiles, 0)),
                  pl.BlockSpec((tm, LANE), lambda i, j: (i % s_tiles, 0))],
        out_specs=pl.BlockSpec((tm, tn), lambda i, j: (i, j)),
        out_shape=jax.ShapeDtypeStruct((m, n), BF16),
        scratch_shapes=[pltpu.VMEM((tm, MLA_Q_RANK), BF16)],
        compiler_params=_params(2),
        name="mla_q_proj",
    )(proj, g.reshape(1, MLA_Q_RANK), w, c1, c2)


def _k_rope_kernel(x_ref, c1_ref, c2_ref, o_ref):
    o_ref[...] = _rope_block(x_ref[...].astype(F32), c1_ref[...], c2_ref[...]).astype(o_ref.dtype)


def _k_rope(proj, c1, c2, seq, tm):
    m = proj.shape[0]
    s_tiles = seq // tm
    return pl.pallas_call(
        _k_rope_kernel,
        grid=(m // tm,),
        in_specs=[pl.BlockSpec((tm, LANE), lambda i: (i, COL_KPE // LANE)),
                  pl.BlockSpec((tm, LANE), lambda i: (i % s_tiles, 0)),
                  pl.BlockSpec((tm, LANE), lambda i: (i % s_tiles, 0))],
        out_specs=pl.BlockSpec((tm, LANE), lambda i: (i, 0)),
        out_shape=jax.ShapeDtypeStruct((m, LANE), BF16),
        compiler_params=_params(1),
        name="mla_k_rope",
    )(proj, c1, c2)


ONES_ROWS = 16


def _chunk_mask_t(t):
    shift = int(math.log2(CHUNK))
    kc = lax.shift_right_logical(lax.broadcasted_iota(jnp.int32, (t, t), 0), shift)
    qc = lax.shift_right_logical(lax.broadcasted_iota(jnp.int32, (t, t), 1), shift)
    return kc <= qc


def _softmax_update(s_t, vt, m_sc, acc_sc, cols=slice(None), col_max=None, shift=None):
    if col_max is None:
        col_max = jnp.max(s_t, axis=0, keepdims=True)
    if shift is not None:
        col_max = col_max + shift
    m_prev = m_sc[:, cols]
    m_new = jnp.maximum(m_prev, col_max)
    alpha = jnp.exp2(m_prev - m_new)
    m_sub = m_new if shift is None else m_new - shift
    p_t = jnp.exp2(s_t - m_sub).astype(BF16)
    v_aug = jnp.concatenate([vt, jnp.ones((ONES_ROWS, vt.shape[1]), BF16)], axis=0)
    acc_sc[:, cols] = alpha * acc_sc[:, cols] + jnp.dot(v_aug, p_t, preferred_element_type=F32)
    m_sc[:, cols] = m_new


def _attend_query_tiles(n_tiles, stages, bufs):
    buf0, buf1 = bufs
    stages.qk(0, 0, *buf0)

    def tile(i, carry):
        stages.init(i)

        def pair(p, carry):
            j = 2 * p
            stages.qk(i, j + 1, *buf1)
            stages.pv(i, j, *buf0)
            stages.qk(i, j + 2, *buf0)
            stages.pv(i, j + 1, *buf1)
            return carry

        lax.fori_loop(0, i, pair, 0)
        j = 2 * i
        stages.qk_diag1(i, j + 1, *buf1)
        stages.diag0(i, j, *buf0)
        stages.qk(jnp.minimum(i + 1, n_tiles - 1), 0, *buf0)
        stages.diag1(i, j + 1, *buf1)
        stages.finalize(i)
        return carry

    lax.fori_loop(0, n_tiles, tile, 0)


class _Stages:
    def __init__(self, **fns):
        self.__dict__.update(fns)


def _mla_attn_kernel(q_ref, kn_ref, kr_ref, vt_ref, o_ref, m_sc, acc_sc, fix_sc, s0_sc, s1_sc,
                     cm0_sc, cm1_sc, *, t):
    tq = 2 * t
    fix_sc[...] = jnp.where(_chunk_mask_t(t), 0.0, NEG_INF)

    def key_rows(j):
        return pl.ds(pl.multiple_of(j * t, t), t)

    def qry_rows(i, lo=0):
        return pl.ds(pl.multiple_of(i * tq + lo, t), tq - lo)

    def keys(j):
        rows = key_rows(j)
        return jnp.concatenate([kn_ref[rows, :], kr_ref[rows, :]], axis=1)

    def init(i):
        m_sc[...] = jnp.full(m_sc.shape, NEG_INF, F32)
        acc_sc[...] = jnp.zeros(acc_sc.shape, F32)

    def qk(i, j, s_sc, cm_sc):
        s_t = lax.dot_general(keys(j), q_ref[qry_rows(i), :], NT_DIMS, preferred_element_type=F32)
        s_sc[...] = s_t
        cm_sc[...] = jnp.max(s_t, axis=0, keepdims=True)

    def qk_diag1(i, j, s_sc, cm_sc):
        s_sc[:, t:] = lax.dot_general(keys(j), q_ref[qry_rows(i, t), :], NT_DIMS,
                                      preferred_element_type=F32)

    def pv(i, j, s_sc, cm_sc):
        _softmax_update(s_sc[...], vt_ref[:, key_rows(j)], m_sc, acc_sc, col_max=cm_sc[...])

    def diag0(i, j, s_sc, cm_sc):
        s_t = jnp.concatenate([s_sc[:, :t] + fix_sc[...], s_sc[:, t:]], axis=1)
        _softmax_update(s_t, vt_ref[:, key_rows(j)], m_sc, acc_sc)

    def diag1(i, j, s_sc, cm_sc):
        _softmax_update(s_sc[:, t:] + fix_sc[...], vt_ref[:, key_rows(j)], m_sc, acc_sc,
                        cols=slice(t, tq))

    def finalize(i):
        acc = acc_sc[...]
        o_t = acc[:MLA_V] / acc[MLA_V:MLA_V + 1]
        o_ref[qry_rows(i), :] = o_t.T.astype(o_ref.dtype)

    stages = _Stages(init=init, qk=qk, qk_diag1=qk_diag1, pv=pv, diag0=diag0, diag1=diag1,
                     finalize=finalize)
    _attend_query_tiles(q_ref.shape[0] // tq, stages, ((s0_sc, cm0_sc), (s1_sc, cm1_sc)))


def _mla_attn(qm, kn, kr, vt, batch, seq, t):
    tq = 2 * t
    return pl.pallas_call(
        functools.partial(_mla_attn_kernel, t=t),
        grid=(batch, MLA_HEADS),
        in_specs=[pl.BlockSpec((seq, MLA_QPAD), lambda b, h: (b, h)),
                  pl.BlockSpec((seq, LANE), lambda b, h: (b, h)),
                  pl.BlockSpec((seq, LANE), lambda b, h: (b, 0)),
                  pl.BlockSpec((MLA_V, seq), lambda b, h: (h, b))],
        out_specs=pl.BlockSpec((seq, MLA_V), lambda b, h: (b, h)),
        out_shape=jax.ShapeDtypeStruct((batch * seq, MLA_HEADS * MLA_V), BF16),
        scratch_shapes=[pltpu.VMEM((1, tq), F32),
                        pltpu.VMEM((MLA_V + ONES_ROWS, tq), F32),
                        pltpu.VMEM((t, t), F32),
                        pltpu.VMEM((t, tq), F32), pltpu.VMEM((t, tq), F32),
                        pltpu.VMEM((1, tq), F32), pltpu.VMEM((1, tq), F32)],
        compiler_params=_params(2),
        name="mla_attn",
    )(qm, kn, kr, vt)


def _diff_attn_kernel(slope_ref, q_ref, k_ref, vt_ref, lq1_ref, lk1_ref, lq2_ref, lk2_ref, sg_ref,
                      o_ref, m_sc, acc_sc, bias_sc, fix_sc, s0_sc, s1_sc, cm0_sc, cm1_sc, *, t,
                      lambda_init):
    d = DIFF_HEAD_DIM
    tq = 2 * t
    slope = slope_ref[pl.program_id(1)]

    key_idx = lax.broadcasted_iota(jnp.int32, (t, t), 0)
    qry_idx = lax.broadcasted_iota(jnp.int32, (t, t), 1)
    bias_sc[...] = key_idx.astype(F32) * slope
    fix = jnp.maximum(key_idx - qry_idx, 0).astype(F32) * (-2.0 * slope)
    fix_sc[...] = jnp.where(_chunk_mask_t(t), fix, NEG_INF)
    lam = (jnp.exp(jnp.sum(lq1_ref[...] * lk1_ref[...], axis=1, keepdims=True))
           - jnp.exp(jnp.sum(lq2_ref[...] * lk2_ref[...], axis=1, keepdims=True))
           + lambda_init)

    def key_rows(j):
        return pl.ds(pl.multiple_of(j * t, t), t)

    def qry_rows(i, lo=0):
        return pl.ds(pl.multiple_of(i * tq + lo, t), tq - lo)

    def scores(i, j, m, lo=0):
        lanes = slice(m * d, (m + 1) * d)
        s = lax.dot_general(k_ref[key_rows(j), lanes], q_ref[qry_rows(i, lo), lanes], NT_DIMS,
                            preferred_element_type=F32)
        return s + jnp.concatenate([bias_sc[...]] * (s.shape[1] // t), axis=1)

    def block_shift(i, j):
        return ((j - 2 * i) * t).astype(F32) * slope

    def init(i):
        m_sc[...] = jnp.full(m_sc.shape, NEG_INF, F32)
        acc_sc[...] = jnp.zeros(acc_sc.shape, F32)

    def qk(i, j, s_sc, cm_sc):
        s_t = jnp.concatenate([scores(i, j, 0), scores(i, j, 1)], axis=1)
        s_sc[...] = s_t
        cm_sc[...] = jnp.max(s_t, axis=0, keepdims=True)

    def qk_diag1(i, j, s_sc, cm_sc):
        s_sc[:, t:tq] = scores(i, j, 0, t)
        s_sc[:, tq + t:] = scores(i, j, 1, t)

    def pv(i, j, s_sc, cm_sc):
        _softmax_update(s_sc[...], vt_ref[:, key_rows(j)], m_sc, acc_sc, col_max=cm_sc[...],
                        shift=block_shift(i, j))

    def diag0(i, j, s_sc, cm_sc):
        fix = fix_sc[...]
        s_t = jnp.concatenate([s_sc[:, :t] + fix, s_sc[:, t:tq], s_sc[:, tq:tq + t] + fix,
                               s_sc[:, tq + t:]], axis=1)
        _softmax_update(s_t, vt_ref[:, key_rows(j)], m_sc, acc_sc, shift=block_shift(i, j))

    def diag1(i, j, s_sc, cm_sc):
        for m in range(2):
            cols = slice(m * tq + t, (m + 1) * tq)
            _softmax_update(s_sc[:, cols] + fix_sc[...], vt_ref[:, key_rows(j)], m_sc, acc_sc,
                            cols=cols, shift=block_shift(i, j))

    def finalize(i):
        acc = acc_sc[...]
        o_t = acc[:DIFF_V] / acc[DIFF_V:DIFF_V + 1]
        o = (o_t[:, :tq] - lam * o_t[:, tq:]).T
        y = _rmsnorm_rows(o, sg_ref[...]) * (1.0 - lambda_init)
        o_ref[qry_rows(i), :] = y.astype(o_ref.dtype)

    stages = _Stages(init=init, qk=qk, qk_diag1=qk_diag1, pv=pv, diag0=diag0, diag1=diag1,
                     finalize=finalize)
    _attend_query_tiles(q_ref.shape[0] // tq, stages, ((s0_sc, cm0_sc), (s1_sc, cm1_sc)))


def _diff_attn(proj, dvt, slopes, lq1, lk1, lq2, lk2, sub_g, batch, seq, t, lambda_init):
    tq = 2 * t
    vec = lambda a: a.reshape(1, -1).astype(F32)
    small = lambda n: pl.BlockSpec((1, n), lambda b, h, s: (0, 0))
    grid_spec = pltpu.PrefetchScalarGridSpec(
        num_scalar_prefetch=1,
        grid=(batch, DIFF_HEADS),
        in_specs=[pl.BlockSpec((seq, DIFF_V), lambda b, h, s: (b, COL_DQ // DIFF_V + h)),
                  pl.BlockSpec((seq, DIFF_V), lambda b, h, s: (b, COL_DK // DIFF_V + h)),
                  pl.BlockSpec((DIFF_V, seq), lambda b, h, s: (h, b)),
                  small(DIFF_HEAD_DIM), small(DIFF_HEAD_DIM), small(DIFF_HEAD_DIM),
                  small(DIFF_HEAD_DIM), small(DIFF_V)],
        out_specs=pl.BlockSpec((seq, DIFF_V), lambda b, h, s: (b, h)),
        scratch_shapes=[pltpu.VMEM((1, 2 * tq), F32),
                        pltpu.VMEM((DIFF_V + ONES_ROWS, 2 * tq), F32),
                        pltpu.VMEM((t, t), F32), pltpu.VMEM((t, t), F32),
                        pltpu.VMEM((t, 2 * tq), F32), pltpu.VMEM((t, 2 * tq), F32),
                        pltpu.VMEM((1, 2 * tq), F32), pltpu.VMEM((1, 2 * tq), F32)],
    )
    return pl.pallas_call(
        functools.partial(_diff_attn_kernel, t=t, lambda_init=lambda_init),
        grid_spec=grid_spec,
        out_shape=jax.ShapeDtypeStruct((batch * seq, DIFF_HEADS * DIFF_V), BF16),
        compiler_params=_params(2),
        name="diff_attn",
    )(slopes, proj, proj, dvt, vec(lq1), vec(lk1), vec(lq2), vec(lk2), vec(sub_g))


def _merge_kernel(ya_ref, yb_ref, wa_ref, wb_ref, ga_ref, gb_ref, ba_ref, bb_ref, o_ref):
    a = jnp.dot(ya_ref[...], wa_ref[...], preferred_element_type=F32)
    b = jnp.dot(yb_ref[...], wb_ref[...], preferred_element_type=F32)
    g0 = jax.nn.sigmoid(ga_ref[...].astype(F32) + ba_ref[...])
    g1 = jax.nn.sigmoid(gb_ref[...].astype(F32) + bb_ref[...])
    o_ref[...] = (g0 * a + g1 * b).astype(o_ref.dtype)


def _merge(y_mla, y_diff, w_mla, w_diff, proj, b_gate, tm, tn):
    m = y_mla.shape[0]
    d = D_MODEL
    g0_blk = COL_GATE // tn
    g1_blk = (COL_GATE + d) // tn
    nb = d // tn
    bg = b_gate.reshape(1, 2 * d).astype(F32)
    return pl.pallas_call(
        _merge_kernel,
        grid=(m // tm, nb),
        in_specs=[pl.BlockSpec((tm, d), lambda i, j: (i, 0)),
                  pl.BlockSpec((tm, d), lambda i, j: (i, 0)),
                  pl.BlockSpec((d, tn), lambda i, j: (0, j)),
                  pl.BlockSpec((d, tn), lambda i, j: (0, j)),
                  pl.BlockSpec((tm, tn), lambda i, j: (i, g0_blk + j)),
                  pl.BlockSpec((tm, tn), lambda i, j: (i, g1_blk + j)),
                  pl.BlockSpec((1, tn), lambda i, j: (0, j)),
                  pl.BlockSpec((1, tn), lambda i, j: (0, nb + j))],
        out_specs=pl.BlockSpec((tm, tn), lambda i, j: (i, j)),
        out_shape=jax.ShapeDtypeStruct((m, d), BF16),
        compiler_params=_params(2),
        name="gated_merge",
    )(y_mla, y_diff, w_mla, w_diff, proj, proj, bg, bg)


def _residual_matmul_kernel(a_ref, w_ref, x_ref, o_ref):
    o_ref[...] = x_ref[...] + jnp.dot(a_ref[...], w_ref[...], preferred_element_type=F32)


def _residual_matmul(a, w, x, tm, tn):
    m, k = a.shape
    n = w.shape[1]
    return pl.pallas_call(
        _residual_matmul_kernel,
        grid=(m // tm, n // tn),
        in_specs=[pl.BlockSpec((tm, k), lambda i, j: (i, 0)),
                  pl.BlockSpec((k, tn), lambda i, j: (0, j)),
                  pl.BlockSpec((tm, tn), lambda i, j: (i, j))],
        out_specs=pl.BlockSpec((tm, tn), lambda i, j: (i, j)),
        out_shape=jax.ShapeDtypeStruct((m, n), F32),
        compiler_params=_params(2),
        name="out_proj_residual",
    )(a, w, x)


def _ffn_up_kernel(x_ref, g_ref, wg_ref, wu_ref, o_ref, h_ref):
    @pl.when(pl.program_id(1) == 0)
    def _():
        h_ref[...] = _rmsnorm_rows(x_ref[...], g_ref[...]).astype(BF16)

    h = h_ref[...]
    gate = jnp.dot(h, wg_ref[...], preferred_element_type=F32)
    up = jnp.dot(h, wu_ref[...], preferred_element_type=F32)
    o_ref[...] = (gate * jax.nn.sigmoid(gate) * up).astype(o_ref.dtype)


def _ffn_up(x, g, wg, wu, tm, tn):
    m, d = x.shape
    n = wg.shape[1]
    return pl.pallas_call(
        _ffn_up_kernel,
        grid=(m // tm, n // tn),
        in_specs=[pl.BlockSpec((tm, d), lambda i, j: (i, 0)),
                  pl.BlockSpec((1, d), lambda i, j: (0, 0)),
                  pl.BlockSpec((d, tn), lambda i, j: (0, j)),
                  pl.BlockSpec((d, tn), lambda i, j: (0, j))],
        out_specs=pl.BlockSpec((tm, tn), lambda i, j: (i, j)),
        out_shape=jax.ShapeDtypeStruct((m, n), BF16),
        scratch_shapes=[pltpu.VMEM((tm, d), BF16)],
        compiler_params=_params(2),
        name="ffn_up",
    )(x, g.reshape(1, d), wg, wu)


def _ffn_down_kernel(a_ref, w_ref, x_ref, g_ref, o_ref, acc_ref, *, final_norm):
    kk = pl.program_id(1)

    @pl.when(kk == 0)
    def _():
        acc_ref[...] = x_ref[...]

    acc_ref[...] += jnp.dot(a_ref[...], w_ref[...], preferred_element_type=F32)

    @pl.when(kk == pl.num_programs(1) - 1)
    def _():
        y = acc_ref[...]
        o_ref[...] = _rmsnorm_rows(y, g_ref[...]) if final_norm else y


def _ffn_down(a, w, x, g, tm, tk, final_norm):
    m, kdim = a.shape
    d = w.shape[1]
    return pl.pallas_call(
        functools.partial(_ffn_down_kernel, final_norm=final_norm),
        grid=(m // tm, kdim // tk),
        in_specs=[pl.BlockSpec((tm, tk), lambda i, k: (i, k)),
                  pl.BlockSpec((tk, d), lambda i, k: (k, 0)),
                  pl.BlockSpec((tm, d), lambda i, k: (i, 0)),
                  pl.BlockSpec((1, d), lambda i, k: (0, 0))],
        out_specs=pl.BlockSpec((tm, d), lambda i, k: (i, 0)),
        out_shape=jax.ShapeDtypeStruct((m, d), F32),
        scratch_shapes=[pltpu.VMEM((tm, d), F32)],
        compiler_params=_params(2),
        name="ffn_down_final_norm",
    )(a, w, x, g.reshape(1, d))


def _rotate_half_cols(w):
    half = w.shape[-1] // 2
    return jnp.concatenate([-w[..., half:], w[..., :half]], axis=-1)


def _prep_w_in(w_in):
    d = D_MODEL
    sizes = (MLA_Q_RANK, MLA_KV_RANK, MLA_ROPE, d, d, d, 2 * d)
    offs = [0]
    for sz in sizes:
        offs.append(offs[-1] + sz)
    cq, ckv, kpe, dq, dk, dv, gate = [w_in[:, offs[n]:offs[n + 1]] for n in range(len(sizes))]
    zeros = lambda n: jnp.zeros((d, n), w_in.dtype)
    dq = dq * (LOG2E / math.sqrt(DIFF_HEAD_DIM))
    cols = [cq, kpe, _rotate_half_cols(kpe), zeros(COL_CKV - COL_KPE - 2 * MLA_ROPE),
            ckv, zeros(COL_GATE - COL_CKV - MLA_KV_RANK), gate, dq, dk]
    return jnp.concatenate(cols, axis=1).astype(BF16), dv.T.astype(BF16)


def _prep_w_uq(w_uq):
    w = w_uq.reshape(MLA_Q_RANK, MLA_HEADS, MLA_QK)
    nope, rope = w[..., :MLA_NOPE], w[..., MLA_NOPE:]
    w = jnp.concatenate([nope, rope, _rotate_half_cols(rope)], axis=-1)
    return w.reshape(MLA_Q_RANK, MLA_HEADS * MLA_QPAD).astype(BF16)


def _prep_w_ukv(w_ukv):
    w = w_ukv.reshape(MLA_KV_RANK, MLA_HEADS, MLA_NOPE + MLA_V)
    k_nope = w[..., :MLA_NOPE].reshape(MLA_KV_RANK, MLA_HEADS * MLA_NOPE)
    v = w[..., MLA_NOPE:].reshape(MLA_KV_RANK, MLA_HEADS * MLA_V)
    return k_nope.astype(BF16), v.T.astype(BF16)


def _rope_tables(seq):
    inv = ROPE_THETA ** (-jnp.arange(0, MLA_ROPE, 2, dtype=F32) / MLA_ROPE)
    ang = jnp.arange(seq, dtype=F32)[:, None] * inv[None, :]
    cos, sin = jnp.cos(ang), jnp.sin(ang)
    zeros = jnp.zeros((seq, LANE - MLA_ROPE), F32)
    return (jnp.concatenate([cos, cos, zeros], axis=1), jnp.concatenate([sin, sin, zeros], axis=1))


def _tile(n, pref):
    t = pref
    while n % t:
        t //= 2
    return t


def kernel(x, attn_norm_g, w_in, b_gate, q_norm_g, w_uq, kv_norm_g, w_ukv, lambda_q1, lambda_k1,
           lambda_q2, lambda_k2, diff_norm_g, w_mla_proj, w_diff_proj, w_out, ffn_norm_g,
           w_ffn_gate, w_ffn_up, w_ffn_down, final_norm_g):
    batch, seq, d = x.shape
    depth = w_in.shape[0]
    assert d == D_MODEL and seq % 1024 == 0
    tokens = batch * seq
    xt = x.reshape(tokens, d)
    c1, c2 = _rope_tables(seq)
    slopes = jnp.exp2(-8.0 * jnp.arange(1, DIFF_HEADS + 1, dtype=F32) / DIFF_HEADS) * LOG2E
    tm = _tile(seq, 1024)
    t_attn = 512
    d_ff = w_ffn_gate.shape[-1]

    for layer in range(depth):
        lambda_init = 0.8 - 0.6 * math.exp(-0.3 * layer)
        w1, w_dv_t = _prep_w_in(w_in[layer])
        w_kn, w_v_t = _prep_w_ukv(w_ukv[layer])
        proj = _norm_matmul(xt, d, 0, attn_norm_g[layer], w1, tm, 1024)
        dvt = _norm_matmul_nt(xt, d, 0, attn_norm_g[layer], w_dv_t, tm, 1024)
        qm = _q_proj(proj, q_norm_g[layer], _prep_w_uq(w_uq[layer]), c1, c2, seq, tm, 1024)
        kn = _norm_matmul(proj, MLA_KV_RANK, COL_CKV // MLA_KV_RANK, kv_norm_g[layer], w_kn,
                          tm, 1024)
        vt = _norm_matmul_nt(proj, MLA_KV_RANK, COL_CKV // MLA_KV_RANK, kv_norm_g[layer], w_v_t,
                             tm, 1024)
        kr = _k_rope(proj, c1, c2, seq, tm)
        y_mla = _mla_attn(qm, kn, kr, vt, batch, seq, t_attn)
        y_diff = _diff_attn(proj, dvt, slopes, lambda_q1[layer], lambda_k1[layer], lambda_q2[layer],
                            lambda_k2[layer], diff_norm_g[layer], batch, seq, t_attn, lambda_init)
        merged = _merge(y_mla, y_diff, w_mla_proj[layer].astype(BF16),
                        w_diff_proj[layer].astype(BF16), proj, b_gate[layer], tm, 512)
        xt = _residual_matmul(merged, w_out[layer].astype(BF16), xt, tm, 1024)
        a = _ffn_up(xt, ffn_norm_g[layer], w_ffn_gate[layer].astype(BF16),
                    w_ffn_up[layer].astype(BF16), tm, 512)
        xt = _ffn_down(a, w_ffn_down[layer].astype(BF16), xt, final_norm_g, 512, d_ff // 4,
                       final_norm=layer == depth - 1)
    return xt.reshape(batch, seq, d)
```

```python
import functools
import math

import jax
import jax.numpy as jnp
from jax import lax
from jax.experimental import pallas as pl
from jax.experimental.pallas import tpu as pltpu

D_MODEL = 2048
CHUNK = 64
EPS = 1e-6
NEG_INF = -1e30
LOG2E = math.log2(math.e)

MLA_HEADS = 16
MLA_Q_RANK = 768
MLA_KV_RANK = 512
MLA_NOPE = 128
MLA_ROPE = 64
MLA_V = 128
MLA_QK = MLA_NOPE + MLA_ROPE
ROPE_THETA = 10000.0

DIFF_HEADS = 8
DIFF_HEAD_DIM = 128
DIFF_V = 2 * DIFF_HEAD_DIM

LANE = 128
MLA_QPAD = 2 * LANE

COL_CQ = 0
COL_KPE = 768
COL_CKV = 1024
COL_GATE = 2048
COL_DQ = COL_GATE + 2 * D_MODEL
COL_DK = COL_DQ + D_MODEL
PROJ_COLS = COL_DK + D_MODEL

VMEM_LIMIT_BYTES = 56 * 1024 * 1024

BF16 = jnp.bfloat16
F32 = jnp.float32
NT_DIMS = (((1,), (1,)), ((), ()))


def _params(n_axes):
    return pltpu.CompilerParams(dimension_semantics=("arbitrary",) * n_axes,
                                vmem_limit_bytes=VMEM_LIMIT_BYTES)


def _rmsnorm_rows(x, g):
    ms = jnp.mean(x * x, axis=-1, keepdims=True)
    return x * lax.rsqrt(ms + EPS) * g


def _norm_matmul_kernel(x_ref, g_ref, w_ref, o_ref, h_ref):
    @pl.when(pl.program_id(1) == 0)
    def _():
        h_ref[...] = _rmsnorm_rows(x_ref[...].astype(F32), g_ref[...]).astype(BF16)

    o_ref[...] = jnp.dot(h_ref[...], w_ref[...], preferred_element_type=F32).astype(o_ref.dtype)


def _norm_matmul(x, x_cols, x_col_block, g, w, tm, tn):
    m = x.shape[0]
    n = w.shape[1]
    return pl.pallas_call(
        _norm_matmul_kernel,
        grid=(m // tm, n // tn),
        in_specs=[pl.BlockSpec((tm, x_cols), lambda i, j: (i, x_col_block)),
                  pl.BlockSpec((1, x_cols), lambda i, j: (0, 0)),
                  pl.BlockSpec((x_cols, tn), lambda i, j: (0, j))],
        out_specs=pl.BlockSpec((tm, tn), lambda i, j: (i, j)),
        out_shape=jax.ShapeDtypeStruct((m, n), BF16),
        scratch_shapes=[pltpu.VMEM((tm, x_cols), BF16)],
        compiler_params=_params(2),
        name="norm_matmul",
    )(x, g.reshape(1, x_cols), w)


def _norm_matmul_nt_kernel(x_ref, g_ref, wt_ref, o_ref, h_ref):
    @pl.when(pl.program_id(1) == 0)
    def _():
        h_ref[...] = _rmsnorm_rows(x_ref[...].astype(F32), g_ref[...]).astype(BF16)

    o_ref[...] = lax.dot_general(wt_ref[...], h_ref[...], NT_DIMS,
                                 preferred_element_type=F32).astype(o_ref.dtype)


def _norm_matmul_nt(x, x_cols, x_col_block, g, wt, tm, tn):
    m = x.shape[0]
    n = wt.shape[0]
    return pl.pallas_call(
        _norm_matmul_nt_kernel,
        grid=(m // tm, n // tn),
        in_specs=[pl.BlockSpec((tm, x_cols), lambda i, j: (i, x_col_block)),
                  pl.BlockSpec((1, x_cols), lambda i, j: (0, 0)),
                  pl.BlockSpec((tn, x_cols), lambda i, j: (j, 0))],
        out_specs=pl.BlockSpec((tn, tm), lambda i, j: (j, i)),
        out_shape=jax.ShapeDtypeStruct((n, m), BF16),
        scratch_shapes=[pltpu.VMEM((tm, x_cols), BF16)],
        compiler_params=_params(2),
        name="norm_matmul_nt",
    )(x, g.reshape(1, x_cols), wt)


def _rope_block(blk, c1, c2):
    return blk * c1 + pltpu.roll(blk, LANE // 2, 1) * c2


def _q_proj_kernel(x_ref, g_ref, w_ref, c1_ref, c2_ref, o_ref, h_ref, *, heads_per_tile, qscale):
    @pl.when(pl.program_id(1) == 0)
    def _():
        h_ref[...] = _rmsnorm_rows(x_ref[...].astype(F32), g_ref[...]).astype(BF16)

    acc = jnp.dot(h_ref[...], w_ref[...], preferred_element_type=F32)
    c1 = c1_ref[...]
    c2 = c2_ref[...]
    for hh in range(heads_per_tile):
        lo = hh * MLA_QPAD
        o_ref[:, lo:lo + LANE] = (acc[:, lo:lo + LANE] * qscale).astype(o_ref.dtype)
        roped = _rope_block(acc[:, lo + LANE:lo + MLA_QPAD], c1, c2)
        o_ref[:, lo + LANE:lo + MLA_QPAD] = (roped * qscale).astype(o_ref.dtype)


def _q_proj(proj, g, w, c1, c2, seq, tm, tn):
    m = proj.shape[0]
    n = w.shape[1]
    s_tiles = seq // tm
    return pl.pallas_call(
        functools.partial(_q_proj_kernel, heads_per_tile=tn // MLA_QPAD,
                          qscale=LOG2E / math.sqrt(MLA_QK)),
        grid=(m // tm, n // tn),
        in_specs=[pl.BlockSpec((tm, MLA_Q_RANK), lambda i, j: (i, COL_CQ // MLA_Q_RANK)),
                  pl.BlockSpec((1, MLA_Q_RANK), lambda i, j: (0, 0)),
                  pl.BlockSpec((MLA_Q_RANK, tn), lambda i, j: (0, j)),
                  pl.BlockSpec((tm, LANE), lambda i, j: (i % s_tiles, 0)),
                  pl.BlockSpec((tm, LANE), lambda i, j: (i % s_tiles, 0))],
        out_specs=pl.BlockSpec((tm, tn), lambda i, j: (i, j)),
        out_shape=jax.ShapeDtypeStruct((m, n), BF16),
        scratch_shapes=[pltpu.VMEM((tm, MLA_Q_RANK), BF16)],
        compiler_params=_params(2),
        name="mla_q_proj",
    )(proj, g.reshape(1, MLA_Q_RANK), w, c1, c2)


def _k_rope_kernel(x_ref, c1_ref, c2_ref, o_ref):
    o_ref[...] = _rope_block(x_ref[...].astype(F32), c1_ref[...], c2_ref[...]).astype(o_ref.dtype)


def _k_rope(proj, c1, c2, seq, tm):
    m = proj.shape[0]
    s_tiles = seq // tm
    return pl.pallas_call(
        _k_rope_kernel,
        grid=(m // tm,),
        in_specs=[pl.BlockSpec((tm, LANE), lambda i: (i, COL_KPE // LANE)),
                  pl.BlockSpec((tm, LANE), lambda i: (i % s_tiles, 0)),
                  pl.BlockSpec((tm, LANE), lambda i: (i % s_tiles, 0))],
        out_specs=pl.BlockSpec((tm, LANE), lambda i: (i, 0)),
        out_shape=jax.ShapeDtypeStruct((m, LANE), BF16),
        compiler_params=_params(1),
        name="mla_k_rope",
    )(proj, c1, c2)


ONES_ROWS = 16
MLA_ONES_ROWS = 128


def _chunk_mask_t(t):
    shift = int(math.log2(CHUNK))
    kc = lax.shift_right_logical(lax.broadcasted_iota(jnp.int32, (t, t), 0), shift)
    qc = lax.shift_right_logical(lax.broadcasted_iota(jnp.int32, (t, t), 1), shift)
    return kc <= qc


def _softmax_update(s_t, vt, m_sc, acc_sc, cols=slice(None), col_max=None, shift=None,
                    ones_rows=ONES_ROWS):
    if col_max is None:
        col_max = jnp.max(s_t, axis=0, keepdims=True)
    if shift is not None:
        col_max = col_max + shift
    m_prev = m_sc[:, cols]
    m_new = jnp.maximum(m_prev, col_max)
    alpha = jnp.exp2(m_prev - m_new)
    m_sub = m_new if shift is None else m_new - shift
    p_t = jnp.exp2(s_t - m_sub).astype(BF16)
    v_aug = jnp.concatenate([vt, jnp.ones((ones_rows, vt.shape[1]), BF16)], axis=0)
    acc_sc[:, cols] = alpha * acc_sc[:, cols] + jnp.dot(v_aug, p_t, preferred_element_type=F32)
    m_sc[:, cols] = m_new


def _attend_query_tiles(n_tiles, stages, bufs):
    buf0, buf1 = bufs
    stages.qk(0, 0, *buf0)

    def tile(i, carry):
        stages.init(i)

        def pair(p, carry):
            j = 2 * p
            stages.qk(i, j + 1, *buf1)
            stages.pv(i, j, *buf0)
            stages.qk(i, j + 2, *buf0)
            stages.pv(i, j + 1, *buf1)
            return carry

        lax.fori_loop(0, i, pair, 0)
        j = 2 * i
        stages.qk_diag1(i, j + 1, *buf1)
        stages.diag0(i, j, *buf0)
        stages.qk(jnp.minimum(i + 1, n_tiles - 1), 0, *buf0)
        stages.diag1(i, j + 1, *buf1)
        stages.finalize(i)
        return carry

    lax.fori_loop(0, n_tiles, tile, 0)


class _Stages:
    def __init__(self, **fns):
        self.__dict__.update(fns)


def _mla_attn_kernel(q_ref, kn_ref, kr_ref, vt_ref, o_ref, m_sc, acc_sc, fix_sc, s0_sc, s1_sc,
                     cm0_sc, cm1_sc, *, t):
    tq = 2 * t
    fix_sc[...] = jnp.where(_chunk_mask_t(t), 0.0, NEG_INF)

    def key_rows(j):
        return pl.ds(pl.multiple_of(j * t, t), t)

    def qry_rows(i, lo=0):
        return pl.ds(pl.multiple_of(i * tq + lo, t), tq - lo)

    def keys(j):
        rows = key_rows(j)
        return jnp.concatenate([kn_ref[rows, :], kr_ref[rows, :]], axis=1)

    def init(i):
        m_sc[...] = jnp.full(m_sc.shape, NEG_INF, F32)
        acc_sc[...] = jnp.zeros(acc_sc.shape, F32)

    def qk(i, j, s_sc, cm_sc):
        s_t = lax.dot_general(keys(j), q_ref[qry_rows(i), :], NT_DIMS, preferred_element_type=F32)
        s_sc[...] = s_t
        cm_sc[...] = jnp.max(s_t, axis=0, keepdims=True)

    def qk_diag1(i, j, s_sc, cm_sc):
        s_sc[:, t:] = lax.dot_general(keys(j), q_ref[qry_rows(i, t), :], NT_DIMS,
                                      preferred_element_type=F32)

    def pv(i, j, s_sc, cm_sc):
        _softmax_update(s_sc[...], vt_ref[:, key_rows(j)], m_sc, acc_sc, col_max=cm_sc[...],
                        ones_rows=MLA_ONES_ROWS)

    def diag0(i, j, s_sc, cm_sc):
        s_t = jnp.concatenate([s_sc[:, :t] + fix_sc[...], s_sc[:, t:]], axis=1)
        _softmax_update(s_t, vt_ref[:, key_rows(j)], m_sc, acc_sc, ones_rows=MLA_ONES_ROWS)

    def diag1(i, j, s_sc, cm_sc):
        _softmax_update(s_sc[:, t:] + fix_sc[...], vt_ref[:, key_rows(j)], m_sc, acc_sc,
                        cols=slice(t, tq), ones_rows=MLA_ONES_ROWS)

    def finalize(i):
        acc = acc_sc[...]
        o_t = acc[:MLA_V] / acc[MLA_V:MLA_V + 1]
        o_ref[qry_rows(i), :] = o_t.T.astype(o_ref.dtype)

    stages = _Stages(init=init, qk=qk, qk_diag1=qk_diag1, pv=pv, diag0=diag0, diag1=diag1,
                     finalize=finalize)
    _attend_query_tiles(q_ref.shape[0] // tq, stages, ((s0_sc, cm0_sc), (s1_sc, cm1_sc)))


def _mla_attn(qm, kn, kr, vt, batch, seq, t):
    tq = 2 * t
    return pl.pallas_call(
        functools.partial(_mla_attn_kernel, t=t),
        grid=(batch, MLA_HEADS),
        in_specs=[pl.BlockSpec((seq, MLA_QPAD), lambda b, h: (b, h)),
                  pl.BlockSpec((seq, LANE), lambda b, h: (b, h)),
                  pl.BlockSpec((seq, LANE), lambda b, h: (b, 0)),
                  pl.BlockSpec((MLA_V, seq), lambda b, h: (h, b))],
        out_specs=pl.BlockSpec((seq, MLA_V), lambda b, h: (b, h)),
        out_shape=jax.ShapeDtypeStruct((batch * seq, MLA_HEADS * MLA_V), BF16),
        scratch_shapes=[pltpu.VMEM((1, tq), F32),
                        pltpu.VMEM((MLA_V + MLA_ONES_ROWS, tq), F32),
                        pltpu.VMEM((t, t), F32),
                        pltpu.VMEM((t, tq), F32), pltpu.VMEM((t, tq), F32),
                        pltpu.VMEM((1, tq), F32), pltpu.VMEM((1, tq), F32)],
        compiler_params=_params(2),
        name="mla_attn",
    )(qm, kn, kr, vt)


def _diff_attn_kernel(slope_ref, q_ref, k_ref, vt_ref, lq1_ref, lk1_ref, lq2_ref, lk2_ref, sg_ref,
                      o_ref, m_sc, acc_sc, bias_sc, fix_sc, s0_sc, s1_sc, cm0_sc, cm1_sc, *, t,
                      lambda_init):
    d = DIFF_HEAD_DIM
    tq = 2 * t
    slope = slope_ref[pl.program_id(1)]

    key_idx = lax.broadcasted_iota(jnp.int32, (t, t), 0)
    qry_idx = lax.broadcasted_iota(jnp.int32, (t, t), 1)
    bias_sc[...] = key_idx.astype(F32) * slope
    fix = jnp.maximum(key_idx - qry_idx, 0).astype(F32) * (-2.0 * slope)
    fix_sc[...] = jnp.where(_chunk_mask_t(t), fix, NEG_INF)
    lam = (jnp.exp(jnp.sum(lq1_ref[...] * lk1_ref[...], axis=1, keepdims=True))
           - jnp.exp(jnp.sum(lq2_ref[...] * lk2_ref[...], axis=1, keepdims=True))
           + lambda_init)

    def key_rows(j):
        return pl.ds(pl.multiple_of(j * t, t), t)

    def qry_rows(i, lo=0):
        return pl.ds(pl.multiple_of(i * tq + lo, t), tq - lo)

    def scores(i, j, m, lo=0):
        lanes = slice(m * d, (m + 1) * d)
        s = lax.dot_general(k_ref[key_rows(j), lanes], q_ref[qry_rows(i, lo), lanes], NT_DIMS,
                            preferred_element_type=F32)
        return s + jnp.concatenate([bias_sc[...]] * (s.shape[1] // t), axis=1)

    def block_shift(i, j):
        return ((j - 2 * i) * t).astype(F32) * slope

    def init(i):
        m_sc[...] = jnp.full(m_sc.shape, NEG_INF, F32)
        acc_sc[...] = jnp.zeros(acc_sc.shape, F32)

    def qk(i, j, s_sc, cm_sc):
        s_t = jnp.concatenate([scores(i, j, 0), scores(i, j, 1)], axis=1)
        s_sc[...] = s_t
        cm_sc[...] = jnp.max(s_t, axis=0, keepdims=True)

    def qk_diag1(i, j, s_sc, cm_sc):
        s_sc[:, t:tq] = scores(i, j, 0, t)
        s_sc[:, tq + t:] = scores(i, j, 1, t)

    def pv(i, j, s_sc, cm_sc):
        _softmax_update(s_sc[...], vt_ref[:, key_rows(j)], m_sc, acc_sc, col_max=cm_sc[...],
                        shift=block_shift(i, j))

    def diag0(i, j, s_sc, cm_sc):
        fix = fix_sc[...]
        s_t = jnp.concatenate([s_sc[:, :t] + fix, s_sc[:, t:tq], s_sc[:, tq:tq + t] + fix,
                               s_sc[:, tq + t:]], axis=1)
        _softmax_update(s_t, vt_ref[:, key_rows(j)], m_sc, acc_sc, shift=block_shift(i, j))

    def diag1(i, j, s_sc, cm_sc):
        for m in range(2):
            cols = slice(m * tq + t, (m + 1) * tq)
            _softmax_update(s_sc[:, cols] + fix_sc[...], vt_ref[:, key_rows(j)], m_sc, acc_sc,
                            cols=cols, shift=block_shift(i, j))

    def finalize(i):
        acc = acc_sc[...]
        o_t = acc[:DIFF_V] / acc[DIFF_V:DIFF_V + 1]
        o = (o_t[:, :tq] - lam * o_t[:, tq:]).T
        y = _rmsnorm_rows(o, sg_ref[...]) * (1.0 - lambda_init)
        o_ref[qry_rows(i), :] = y.astype(o_ref.dtype)

    stages = _Stages(init=init, qk=qk, qk_diag1=qk_diag1, pv=pv, diag0=diag0, diag1=diag1,
                     finalize=finalize)
    _attend_query_tiles(q_ref.shape[0] // tq, stages, ((s0_sc, cm0_sc), (s1_sc, cm1_sc)))


def _diff_attn(proj, dvt, slopes, lq1, lk1, lq2, lk2, sub_g, batch, seq, t, lambda_init):
    tq = 2 * t
    vec = lambda a: a.reshape(1, -1).astype(F32)
    small = lambda n: pl.BlockSpec((1, n), lambda b, h, s: (0, 0))
    grid_spec = pltpu.PrefetchScalarGridSpec(
        num_scalar_prefetch=1,
        grid=(batch, DIFF_HEADS),
        in_specs=[pl.BlockSpec((seq, DIFF_V), lambda b, h, s: (b, COL_DQ // DIFF_V + h)),
                  pl.BlockSpec((seq, DIFF_V), lambda b, h, s: (b, COL_DK // DIFF_V + h)),
                  pl.BlockSpec((DIFF_V, seq), lambda b, h, s: (h, b)),
                  small(DIFF_HEAD_DIM), small(DIFF_HEAD_DIM), small(DIFF_HEAD_DIM),
                  small(DIFF_HEAD_DIM), small(DIFF_V)],
        out_specs=pl.BlockSpec((seq, DIFF_V), lambda b, h, s: (b, h)),
        scratch_shapes=[pltpu.VMEM((1, 2 * tq), F32),
                        pltpu.VMEM((DIFF_V + ONES_ROWS, 2 * tq), F32),
                        pltpu.VMEM((t, t), F32), pltpu.VMEM((t, t), F32),
                        pltpu.VMEM((t, 2 * tq), F32), pltpu.VMEM((t, 2 * tq), F32),
                        pltpu.VMEM((1, 2 * tq), F32), pltpu.VMEM((1, 2 * tq), F32)],
    )
    return pl.pallas_call(
        functools.partial(_diff_attn_kernel, t=t, lambda_init=lambda_init),
        grid_spec=grid_spec,
        out_shape=jax.ShapeDtypeStruct((batch * seq, DIFF_HEADS * DIFF_V), BF16),
        compiler_params=_params(2),
        name="diff_attn",
    )(slopes, proj, proj, dvt, vec(lq1), vec(lk1), vec(lq2), vec(lk2), vec(sub_g))


def _merge_kernel(ya_ref, yb_ref, wa_ref, wb_ref, ga_ref, gb_ref, ba_ref, bb_ref, o_ref):
    a = jnp.dot(ya_ref[...], wa_ref[...], preferred_element_type=F32)
    b = jnp.dot(yb_ref[...], wb_ref[...], preferred_element_type=F32)
    g0 = jax.nn.sigmoid(ga_ref[...].astype(F32) + ba_ref[...])
    g1 = jax.nn.sigmoid(gb_ref[...].astype(F32) + bb_ref[...])
    o_ref[...] = (g0 * a + g1 * b).astype(o_ref.dtype)


def _merge(y_mla, y_diff, w_mla, w_diff, proj, b_gate, tm, tn):
    m = y_mla.shape[0]
    d = D_MODEL
    g0_blk = COL_GATE // tn
    g1_blk = (COL_GATE + d) // tn
    nb = d // tn
    bg = b_gate.reshape(1, 2 * d).astype(F32)
    return pl.pallas_call(
        _merge_kernel,
        grid=(m // tm, nb),
        in_specs=[pl.BlockSpec((tm, d), lambda i, j: (i, 0)),
                  pl.BlockSpec((tm, d), lambda i, j: (i, 0)),
                  pl.BlockSpec((d, tn), lambda i, j: (0, j)),
                  pl.BlockSpec((d, tn), lambda i, j: (0, j)),
                  pl.BlockSpec((tm, tn), lambda i, j: (i, g0_blk + j)),
                  pl.BlockSpec((tm, tn), lambda i, j: (i, g1_blk + j)),
                  pl.BlockSpec((1, tn), lambda i, j: (0, j)),
                  pl.BlockSpec((1, tn), lambda i, j: (0, nb + j))],
        out_specs=pl.BlockSpec((tm, tn), lambda i, j: (i, j)),
        out_shape=jax.ShapeDtypeStruct((m, d), BF16),
        compiler_params=_params(2),
        name="gated_merge",
    )(y_mla, y_diff, w_mla, w_diff, proj, proj, bg, bg)


def _residual_matmul_kernel(a_ref, w_ref, x_ref, o_ref):
    o_ref[...] = x_ref[...] + jnp.dot(a_ref[...], w_ref[...], preferred_element_type=F32)


def _residual_matmul(a, w, x, tm, tn):
    m, k = a.shape
    n = w.shape[1]
    return pl.pallas_call(
        _residual_matmul_kernel,
        grid=(m // tm, n // tn),
        in_specs=[pl.BlockSpec((tm, k), lambda i, j: (i, 0)),
                  pl.BlockSpec((k, tn), lambda i, j: (0, j)),
                  pl.BlockSpec((tm, tn), lambda i, j: (i, j))],
        out_specs=pl.BlockSpec((tm, tn), lambda i, j: (i, j)),
        out_shape=jax.ShapeDtypeStruct((m, n), F32),
        compiler_params=_params(2),
        name="out_proj_residual",
    )(a, w, x)


def _ffn_up_kernel(x_ref, g_ref, wg_ref, wu_ref, o_ref, h_ref):
    @pl.when(pl.program_id(1) == 0)
    def _():
        h_ref[...] = _rmsnorm_rows(x_ref[...], g_ref[...]).astype(BF16)

    h = h_ref[...]
    gate = jnp.dot(h, wg_ref[...], preferred_element_type=F32)
    up = jnp.dot(h, wu_ref[...], preferred_element_type=F32)
    o_ref[...] = (gate * jax.nn.sigmoid(gate) * up).astype(o_ref.dtype)


def _ffn_up(x, g, wg, wu, tm, tn):
    m, d = x.shape
    n = wg.shape[1]
    return pl.pallas_call(
        _ffn_up_kernel,
        grid=(m // tm, n // tn),
        in_specs=[pl.BlockSpec((tm, d), lambda i, j: (i, 0)),
                  pl.BlockSpec((1, d), lambda i, j: (0, 0)),
                  pl.BlockSpec((d, tn), lambda i, j: (0, j)),
                  pl.BlockSpec((d, tn), lambda i, j: (0, j))],
        out_specs=pl.BlockSpec((tm, tn), lambda i, j: (i, j)),
        out_shape=jax.ShapeDtypeStruct((m, n), BF16),
        scratch_shapes=[pltpu.VMEM((tm, d), BF16)],
        compiler_params=_params(2),
        name="ffn_up",
    )(x, g.reshape(1, d), wg, wu)


def _ffn_down_kernel(a_ref, w_ref, x_ref, g_ref, o_ref, *, final_norm):
    y = x_ref[...] + jnp.dot(a_ref[...], w_ref[...], preferred_element_type=F32)
    o_ref[...] = _rmsnorm_rows(y, g_ref[...]) if final_norm else y


def _ffn_down(a, w, x, g, tm, final_norm):
    m, kdim = a.shape
    d = w.shape[1]
    return pl.pallas_call(
        functools.partial(_ffn_down_kernel, final_norm=final_norm),
        grid=(m // tm,),
        in_specs=[pl.BlockSpec((tm, kdim), lambda i: (i, 0)),
                  pl.BlockSpec((kdim, d), lambda i: (0, 0), pipeline_mode=pl.Buffered(1)),
                  pl.BlockSpec((tm, d), lambda i: (i, 0)),
                  pl.BlockSpec((1, d), lambda i: (0, 0))],
        out_specs=pl.BlockSpec((tm, d), lambda i: (i, 0)),
        out_shape=jax.ShapeDtypeStruct((m, d), F32),
        compiler_params=_params(1),
        name="ffn_down_final_norm",
    )(a, w, x, g.reshape(1, d))


def _rotate_half_cols(w):
    half = w.shape[-1] // 2
    return jnp.concatenate([-w[..., half:], w[..., :half]], axis=-1)


def _prep_w_in(w_in):
    d = D_MODEL
    sizes = (MLA_Q_RANK, MLA_KV_RANK, MLA_ROPE, d, d, d, 2 * d)
    offs = [0]
    for sz in sizes:
        offs.append(offs[-1] + sz)
    cq, ckv, kpe, dq, dk, dv, gate = [w_in[:, offs[n]:offs[n + 1]] for n in range(len(sizes))]
    zeros = lambda n: jnp.zeros((d, n), w_in.dtype)
    dq = dq * (LOG2E / math.sqrt(DIFF_HEAD_DIM))
    cols = [cq, kpe, _rotate_half_cols(kpe), zeros(COL_CKV - COL_KPE - 2 * MLA_ROPE),
            ckv, zeros(COL_GATE - COL_CKV - MLA_KV_RANK), gate, dq, dk]
    return jnp.concatenate(cols, axis=1).astype(BF16), dv.T.astype(BF16)


def _prep_w_uq(w_uq):
    w = w_uq.reshape(MLA_Q_RANK, MLA_HEADS, MLA_QK)
    nope, rope = w[..., :MLA_NOPE], w[..., MLA_NOPE:]
    w = jnp.concatenate([nope, rope, _rotate_half_cols(rope)], axis=-1)
    return w.reshape(MLA_Q_RANK, MLA_HEADS * MLA_QPAD).astype(BF16)


def _prep_w_ukv(w_ukv):
    w = w_ukv.reshape(MLA_KV_RANK, MLA_HEADS, MLA_NOPE + MLA_V)
    k_nope = w[..., :MLA_NOPE].reshape(MLA_KV_RANK, MLA_HEADS * MLA_NOPE)
    v = w[..., MLA_NOPE:].reshape(MLA_KV_RANK, MLA_HEADS * MLA_V)
    return k_nope.astype(BF16), v.T.astype(BF16)


def _rope_tables(seq):
    inv = ROPE_THETA ** (-jnp.arange(0, MLA_ROPE, 2, dtype=F32) / MLA_ROPE)
    ang = jnp.arange(seq, dtype=F32)[:, None] * inv[None, :]
    cos, sin = jnp.cos(ang), jnp.sin(ang)
    zeros = jnp.zeros((seq, LANE - MLA_ROPE), F32)
    return (jnp.concatenate([cos, cos, zeros], axis=1), jnp.concatenate([sin, sin, zeros], axis=1))


def _tile(n, pref):
    t = pref
    while n % t:
        t //= 2
    return t


def kernel(x, attn_norm_g, w_in, b_gate, q_norm_g, w_uq, kv_norm_g, w_ukv, lambda_q1, lambda_k1,
           lambda_q2, lambda_k2, diff_norm_g, w_mla_proj, w_diff_proj, w_out, ffn_norm_g,
           w_ffn_gate, w_ffn_up, w_ffn_down, final_norm_g):
    batch, seq, d = x.shape
    depth = w_in.shape[0]
    assert d == D_MODEL and seq % 1024 == 0
    tokens = batch * seq
    xt = x.reshape(tokens, d)
    c1, c2 = _rope_tables(seq)
    slopes = jnp.exp2(-8.0 * jnp.arange(1, DIFF_HEADS + 1, dtype=F32) / DIFF_HEADS) * LOG2E
    tm = _tile(seq, 1024)
    t_attn = 512

    for layer in range(depth):
        lambda_init = 0.8 - 0.6 * math.exp(-0.3 * layer)
        w1, w_dv_t = _prep_w_in(w_in[layer])
        w_kn, w_v_t = _prep_w_ukv(w_ukv[layer])
        proj = _norm_matmul(xt, d, 0, attn_norm_g[layer], w1, tm, 1024)
        dvt = _norm_matmul_nt(xt, d, 0, attn_norm_g[layer], w_dv_t, tm, 1024)
        qm = _q_proj(proj, q_norm_g[layer], _prep_w_uq(w_uq[layer]), c1, c2, seq, tm, 1024)
        kn = _norm_matmul(proj, MLA_KV_RANK, COL_CKV // MLA_KV_RANK, kv_norm_g[layer], w_kn,
                          tm, 1024)
        vt = _norm_matmul_nt(proj, MLA_KV_RANK, COL_CKV // MLA_KV_RANK, kv_norm_g[layer], w_v_t,
                             tm, 1024)
        kr = _k_rope(proj, c1, c2, seq, tm)
        y_mla = _mla_attn(qm, kn, kr, vt, batch, seq, t_attn)
        y_diff = _diff_attn(proj, dvt, slopes, lambda_q1[layer], lambda_k1[layer], lambda_q2[layer],
                            lambda_k2[layer], diff_norm_g[layer], batch, seq, t_attn, lambda_init)
        merged = _merge(y_mla, y_diff, w_mla_proj[layer].astype(BF16),
                        w_diff_proj[layer].astype(BF16), proj, b_gate[layer], tm, 1024)
        xt = _residual_matmul(merged, w_out[layer].astype(BF16), xt, tm, 1024)
        a = _ffn_up(xt, ffn_norm_g[layer], w_ffn_gate[layer].astype(BF16),
                    w_ffn_up[layer].astype(BF16), tm, 512)
        xt = _ffn_down(a, w_ffn_down[layer].astype(BF16), xt, final_norm_g, 512,
                       final_norm=layer == depth - 1)
    return xt.reshape(batch, seq, d)
```

```python
import functools
import math

import jax
import jax.numpy as jnp
from jax import lax
from jax.experimental import pallas as pl
from jax.experimental.pallas import tpu as pltpu

D_MODEL = 2048
CHUNK = 64
EPS = 1e-6
NEG_INF = -1e30
LOG2E = math.log2(math.e)

MLA_HEADS = 16
MLA_Q_RANK = 768
MLA_KV_RANK = 512
MLA_NOPE = 128
MLA_ROPE = 64
MLA_V = 128
MLA_QK = MLA_NOPE + MLA_ROPE
ROPE_THETA = 10000.0

DIFF_HEADS = 8
DIFF_HEAD_DIM = 128
DIFF_V = 2 * DIFF_HEAD_DIM

LANE = 128
MLA_QPAD = 2 * LANE

COL_CQ = 0
COL_KPE = 768
COL_CKV = 1024
COL_GATE = 2048
COL_DQ = COL_GATE + 2 * D_MODEL
COL_DK = COL_DQ + D_MODEL
PROJ_COLS = COL_DK + D_MODEL

VMEM_LIMIT_BYTES = 56 * 1024 * 1024

BF16 = jnp.bfloat16
F32 = jnp.float32
NT_DIMS = (((1,), (1,)), ((), ()))


def _params(n_axes):
    return pltpu.CompilerParams(dimension_semantics=("arbitrary",) * n_axes,
                                vmem_limit_bytes=VMEM_LIMIT_BYTES)


def _rmsnorm_rows(x, g):
    ms = jnp.mean(x * x, axis=-1, keepdims=True)
    return x * lax.rsqrt(ms + EPS) * g


def _norm_matmul_kernel(x_ref, g_ref, w_ref, o_ref, h_ref):
    @pl.when(pl.program_id(1) == 0)
    def _():
        h_ref[...] = _rmsnorm_rows(x_ref[...].astype(F32), g_ref[...]).astype(BF16)

    o_ref[...] = jnp.dot(h_ref[...], w_ref[...], preferred_element_type=F32).astype(o_ref.dtype)


def _norm_matmul(x, x_cols, x_col_block, g, w, tm, tn):
    m = x.shape[0]
    n = w.shape[1]
    return pl.pallas_call(
        _norm_matmul_kernel,
        grid=(m // tm, n // tn),
        in_specs=[pl.BlockSpec((tm, x_cols), lambda i, j: (i, x_col_block)),
                  pl.BlockSpec((1, x_cols), lambda i, j: (0, 0)),
                  pl.BlockSpec((x_cols, tn), lambda i, j: (0, j))],
        out_specs=pl.BlockSpec((tm, tn), lambda i, j: (i, j)),
        out_shape=jax.ShapeDtypeStruct((m, n), BF16),
        scratch_shapes=[pltpu.VMEM((tm, x_cols), BF16)],
        compiler_params=_params(2),
        name="norm_matmul",
    )(x, g.reshape(1, x_cols), w)


def _norm_matmul_nt_kernel(x_ref, g_ref, wt_ref, o_ref, h_ref):
    @pl.when(pl.program_id(1) == 0)
    def _():
        h_ref[...] = _rmsnorm_rows(x_ref[...].astype(F32), g_ref[...]).astype(BF16)

    o_ref[...] = lax.dot_general(wt_ref[...], h_ref[...], NT_DIMS,
                                 preferred_element_type=F32).astype(o_ref.dtype)


def _norm_matmul_nt(x, x_cols, x_col_block, g, wt, tm, tn):
    m = x.shape[0]
    n = wt.shape[0]
    return pl.pallas_call(
        _norm_matmul_nt_kernel,
        grid=(m // tm, n // tn),
        in_specs=[pl.BlockSpec((tm, x_cols), lambda i, j: (i, x_col_block)),
                  pl.BlockSpec((1, x_cols), lambda i, j: (0, 0)),
                  pl.BlockSpec((tn, x_cols), lambda i, j: (j, 0))],
        out_specs=pl.BlockSpec((tn, tm), lambda i, j: (j, i)),
        out_shape=jax.ShapeDtypeStruct((n, m), BF16),
        scratch_shapes=[pltpu.VMEM((tm, x_cols), BF16)],
        compiler_params=_params(2),
        name="norm_matmul_nt",
    )(x, g.reshape(1, x_cols), wt)


def _rope_block(blk, c1, c2):
    return blk * c1 + pltpu.roll(blk, LANE // 2, 1) * c2


def _q_proj_kernel(x_ref, g_ref, w_ref, c1_ref, c2_ref, o_ref, h_ref, *, heads_per_tile, qscale):
    @pl.when(pl.program_id(1) == 0)
    def _():
        h_ref[...] = _rmsnorm_rows(x_ref[...].astype(F32), g_ref[...]).astype(BF16)

    acc = jnp.dot(h_ref[...], w_ref[...], preferred_element_type=F32)
    c1 = c1_ref[...]
    c2 = c2_ref[...]
    for hh in range(heads_per_tile):
        lo = hh * MLA_QPAD
        o_ref[:, lo:lo + LANE] = (acc[:, lo:lo + LANE] * qscale).astype(o_ref.dtype)
        roped = _rope_block(acc[:, lo + LANE:lo + MLA_QPAD], c1, c2)
        o_ref[:, lo + LANE:lo + MLA_QPAD] = (roped * qscale).astype(o_ref.dtype)


def _q_proj(proj, g, w, c1, c2, seq, tm, tn):
    m = proj.shape[0]
    n = w.shape[1]
    s_tiles = seq // tm
    return pl.pallas_call(
        functools.partial(_q_proj_kernel, heads_per_tile=tn // MLA_QPAD,
                          qscale=LOG2E / math.sqrt(MLA_QK)),
        grid=(m // tm, n // tn),
        in_specs=[pl.BlockSpec((tm, MLA_Q_RANK), lambda i, j: (i, COL_CQ // MLA_Q_RANK)),
                  pl.BlockSpec((1, MLA_Q_RANK), lambda i, j: (0, 0)),
                  pl.BlockSpec((MLA_Q_RANK, tn), lambda i, j: (0, j)),
                  pl.BlockSpec((tm, LANE), lambda i, j: (i % s_tiles, 0)),
                  pl.BlockSpec((tm, LANE), lambda i, j: (i % s_tiles, 0))],
        out_specs=pl.BlockSpec((tm, tn), lambda i, j: (i, j)),
        out_shape=jax.ShapeDtypeStruct((m, n), BF16),
        scratch_shapes=[pltpu.VMEM((tm, MLA_Q_RANK), BF16)],
        compiler_params=_params(2),
        name="mla_q_proj",
    )(proj, g.reshape(1, MLA_Q_RANK), w, c1, c2)


def _k_rope_kernel(x_ref, c1_ref, c2_ref, o_ref):
    o_ref[...] = _rope_block(x_ref[...].astype(F32), c1_ref[...], c2_ref[...]).astype(o_ref.dtype)


def _k_rope(proj, c1, c2, seq, tm):
    m = proj.shape[0]
    s_tiles = seq // tm
    return pl.pallas_call(
        _k_rope_kernel,
        grid=(m // tm,),
        in_specs=[pl.BlockSpec((tm, LANE), lambda i: (i, COL_KPE // LANE)),
                  pl.BlockSpec((tm, LANE), lambda i: (i % s_tiles, 0)),
                  pl.BlockSpec((tm, LANE), lambda i: (i % s_tiles, 0))],
        out_specs=pl.BlockSpec((tm, LANE), lambda i: (i, 0)),
        out_shape=jax.ShapeDtypeStruct((m, LANE), BF16),
        compiler_params=_params(1),
        name="mla_k_rope",
    )(proj, c1, c2)


ONES_ROWS = 16
MLA_ONES_ROWS = 16


def _chunk_mask_t(t):
    shift = int(math.log2(CHUNK))
    kc = lax.shift_right_logical(lax.broadcasted_iota(jnp.int32, (t, t), 0), shift)
    qc = lax.shift_right_logical(lax.broadcasted_iota(jnp.int32, (t, t), 1), shift)
    return kc <= qc


def _softmax_update(s_t, vt, m_sc, acc_sc, cols=slice(None), col_max=None, shift=None,
                    ones_rows=ONES_ROWS):
    if col_max is None:
        col_max = jnp.max(s_t, axis=0, keepdims=True)
    if shift is not None:
        col_max = col_max + shift
    m_prev = m_sc[:, cols]
    m_new = jnp.maximum(m_prev, col_max)
    alpha = jnp.exp2(m_prev - m_new)
    m_sub = m_new if shift is None else m_new - shift
    p_t = jnp.exp2(s_t - m_sub).astype(BF16)
    v_aug = jnp.concatenate([vt, jnp.ones((ones_rows, vt.shape[1]), BF16)], axis=0)
    acc_sc[:, cols] = alpha * acc_sc[:, cols] + jnp.dot(v_aug, p_t, preferred_element_type=F32)
    m_sc[:, cols] = m_new


def _attend_query_tiles(n_tiles, stages, bufs, pairs_per_trip=1):
    buf0, buf1 = bufs
    stages.qk(0, 0, *buf0)

    def tile(i, carry):
        stages.init(i)

        def pair(p):
            j = 2 * p
            stages.qk(i, j + 1, *buf1)
            stages.pv(i, j, *buf0)
            stages.qk(i, j + 2, *buf0)
            stages.pv(i, j + 1, *buf1)

        def trip(p, carry):
            for u in range(pairs_per_trip):
                pair(pairs_per_trip * p + u)
            return carry

        def single(p, carry):
            pair(p)
            return carry

        n_trips = lax.div(i, pairs_per_trip)
        lax.fori_loop(0, n_trips, trip, 0)
        if pairs_per_trip > 1:
            lax.fori_loop(n_trips * pairs_per_trip, i, single, 0)
        j = 2 * i
        stages.qk_diag1(i, j + 1, *buf1)
        stages.diag0(i, j, *buf0)
        stages.qk(jnp.minimum(i + 1, n_tiles - 1), 0, *buf0)
        stages.diag1(i, j + 1, *buf1)
        stages.finalize(i)
        return carry

    lax.fori_loop(0, n_tiles, tile, 0)


class _Stages:
    def __init__(self, **fns):
        self.__dict__.update(fns)


def _mla_attn_kernel(q_ref, kn_ref, kr_ref, vt_ref, o_ref, m_sc, acc_sc, fix_sc, s0_sc, s1_sc,
                     cm0_sc, cm1_sc, *, t):
    tq = 2 * t
    fix_sc[...] = jnp.where(_chunk_mask_t(t), 0.0, NEG_INF)

    def key_rows(j):
        return pl.ds(pl.multiple_of(j * t, t), t)

    def qry_rows(i, lo=0):
        return pl.ds(pl.multiple_of(i * tq + lo, t), tq - lo)

    def keys(j):
        rows = key_rows(j)
        return jnp.concatenate([kn_ref[rows, :], kr_ref[rows, :]], axis=1)

    def init(i):
        m_sc[...] = jnp.full(m_sc.shape, NEG_INF, F32)
        acc_sc[...] = jnp.zeros(acc_sc.shape, F32)

    def qk(i, j, s_sc, cm_sc):
        s_t = lax.dot_general(keys(j), q_ref[qry_rows(i), :], NT_DIMS, preferred_element_type=F32)
        s_sc[...] = s_t
        cm_sc[...] = jnp.max(s_t, axis=0, keepdims=True)

    def qk_diag1(i, j, s_sc, cm_sc):
        s_sc[:, t:] = lax.dot_general(keys(j), q_ref[qry_rows(i, t), :], NT_DIMS,
                                      preferred_element_type=F32)

    def pv(i, j, s_sc, cm_sc):
        _softmax_update(s_sc[...], vt_ref[:, key_rows(j)], m_sc, acc_sc, col_max=cm_sc[...],
                        ones_rows=MLA_ONES_ROWS)

    def diag0(i, j, s_sc, cm_sc):
        s_t = jnp.concatenate([s_sc[:, :t] + fix_sc[...], s_sc[:, t:]], axis=1)
        _softmax_update(s_t, vt_ref[:, key_rows(j)], m_sc, acc_sc, ones_rows=MLA_ONES_ROWS)

    def diag1(i, j, s_sc, cm_sc):
        _softmax_update(s_sc[:, t:] + fix_sc[...], vt_ref[:, key_rows(j)], m_sc, acc_sc,
                        cols=slice(t, tq), ones_rows=MLA_ONES_ROWS)

    def finalize(i):
        acc = acc_sc[...]
        o_t = acc[:MLA_V] / acc[MLA_V:MLA_V + 1]
        o_ref[qry_rows(i), :] = o_t.T.astype(o_ref.dtype)

    stages = _Stages(init=init, qk=qk, qk_diag1=qk_diag1, pv=pv, diag0=diag0, diag1=diag1,
                     finalize=finalize)
    _attend_query_tiles(q_ref.shape[0] // tq, stages, ((s0_sc, cm0_sc), (s1_sc, cm1_sc)),
                        pairs_per_trip=2)


def _mla_attn(qm, kn, kr, vt, batch, seq, t):
    tq = 2 * t
    return pl.pallas_call(
        functools.partial(_mla_attn_kernel, t=t),
        grid=(batch, MLA_HEADS),
        in_specs=[pl.BlockSpec((seq, MLA_QPAD), lambda b, h: (b, h)),
                  pl.BlockSpec((seq, LANE), lambda b, h: (b, h)),
                  pl.BlockSpec((seq, LANE), lambda b, h: (b, 0)),
                  pl.BlockSpec((MLA_V, seq), lambda b, h: (h, b))],
        out_specs=pl.BlockSpec((seq, MLA_V), lambda b, h: (b, h)),
        out_shape=jax.ShapeDtypeStruct((batch * seq, MLA_HEADS * MLA_V), BF16),
        scratch_shapes=[pltpu.VMEM((1, tq), F32),
                        pltpu.VMEM((MLA_V + MLA_ONES_ROWS, tq), F32),
                        pltpu.VMEM((t, t), F32),
                        pltpu.VMEM((t, tq), F32), pltpu.VMEM((t, tq), F32),
                        pltpu.VMEM((1, tq), F32), pltpu.VMEM((1, tq), F32)],
        compiler_params=_params(2),
        name="mla_attn",
    )(qm, kn, kr, vt)


def _diff_attn_kernel(slope_ref, q_ref, k_ref, vt_ref, lq1_ref, lk1_ref, lq2_ref, lk2_ref, sg_ref,
                      o_ref, m_sc, acc_sc, bias_sc, fix_sc, s0_sc, s1_sc, cm0_sc, cm1_sc, *, t,
                      lambda_init):
    d = DIFF_HEAD_DIM
    tq = 2 * t
    slope = slope_ref[pl.program_id(1)]

    key_idx = lax.broadcasted_iota(jnp.int32, (t, t), 0)
    qry_idx = lax.broadcasted_iota(jnp.int32, (t, t), 1)
    bias_sc[...] = key_idx.astype(F32) * slope
    fix = jnp.maximum(key_idx - qry_idx, 0).astype(F32) * (-2.0 * slope)
    fix_sc[...] = jnp.where(_chunk_mask_t(t), fix, NEG_INF)
    lam = (jnp.exp(jnp.sum(lq1_ref[...] * lk1_ref[...], axis=1, keepdims=True))
           - jnp.exp(jnp.sum(lq2_ref[...] * lk2_ref[...], axis=1, keepdims=True))
           + lambda_init)

    def key_rows(j):
        return pl.ds(pl.multiple_of(j * t, t), t)

    def qry_rows(i, lo=0):
        return pl.ds(pl.multiple_of(i * tq + lo, t), tq - lo)

    def scores(i, j, m, lo=0):
        lanes = slice(m * d, (m + 1) * d)
        s = lax.dot_general(k_ref[key_rows(j), lanes], q_ref[qry_rows(i, lo), lanes], NT_DIMS,
                            preferred_element_type=F32)
        return s + jnp.concatenate([bias_sc[...]] * (s.shape[1] // t), axis=1)

    def block_shift(i, j):
        return lax.convert_element_type((j - 2 * i) * t, F32) * slope

    def init(i):
        m_sc[...] = jnp.full(m_sc.shape, NEG_INF, F32)
        acc_sc[...] = jnp.zeros(acc_sc.shape, F32)

    def qk(i, j, s_sc, cm_sc):
        s_t = jnp.concatenate([scores(i, j, 0), scores(i, j, 1)], axis=1)
        s_sc[...] = s_t
        cm_sc[...] = jnp.max(s_t, axis=0, keepdims=True)

    def qk_diag1(i, j, s_sc, cm_sc):
        s_sc[:, t:tq] = scores(i, j, 0, t)
        s_sc[:, tq + t:] = scores(i, j, 1, t)

    def pv(i, j, s_sc, cm_sc):
        _softmax_update(s_sc[...], vt_ref[:, key_rows(j)], m_sc, acc_sc, col_max=cm_sc[...],
                        shift=block_shift(i, j))

    def diag0(i, j, s_sc, cm_sc):
        fix = fix_sc[...]
        s_t = jnp.concatenate([s_sc[:, :t] + fix, s_sc[:, t:tq], s_sc[:, tq:tq + t] + fix,
                               s_sc[:, tq + t:]], axis=1)
        _softmax_update(s_t, vt_ref[:, key_rows(j)], m_sc, acc_sc, shift=block_shift(i, j))

    def diag1(i, j, s_sc, cm_sc):
        for m in range(2):
            cols = slice(m * tq + t, (m + 1) * tq)
            _softmax_update(s_sc[:, cols] + fix_sc[...], vt_ref[:, key_rows(j)], m_sc, acc_sc,
                            cols=cols, shift=block_shift(i, j))

    def finalize(i):
        acc = acc_sc[...]
        o_t = acc[:DIFF_V] / acc[DIFF_V:DIFF_V + 1]
        o = (o_t[:, :tq] - lam * o_t[:, tq:]).T
        y = _rmsnorm_rows(o, sg_ref[...]) * (1.0 - lambda_init)
        o_ref[qry_rows(i), :] = y.astype(o_ref.dtype)

    stages = _Stages(init=init, qk=qk, qk_diag1=qk_diag1, pv=pv, diag0=diag0, diag1=diag1,
                     finalize=finalize)
    _attend_query_tiles(q_ref.shape[0] // tq, stages, ((s0_sc, cm0_sc), (s1_sc, cm1_sc)))


def _diff_attn(proj, dvt, slopes, lq1, lk1, lq2, lk2, sub_g, batch, seq, t, lambda_init):
    tq = 2 * t
    vec = lambda a: a.reshape(1, -1).astype(F32)
    small = lambda n: pl.BlockSpec((1, n), lambda b, h, s: (0, 0))
    grid_spec = pltpu.PrefetchScalarGridSpec(
        num_scalar_prefetch=1,
        grid=(batch, DIFF_HEADS),
        in_specs=[pl.BlockSpec((seq, DIFF_V), lambda b, h, s: (b, COL_DQ // DIFF_V + h)),
                  pl.BlockSpec((seq, DIFF_V), lambda b, h, s: (b, COL_DK // DIFF_V + h)),
                  pl.BlockSpec((DIFF_V, seq), lambda b, h, s: (h, b)),
                  small(DIFF_HEAD_DIM), small(DIFF_HEAD_DIM), small(DIFF_HEAD_DIM),
                  small(DIFF_HEAD_DIM), small(DIFF_V)],
        out_specs=pl.BlockSpec((seq, DIFF_V), lambda b, h, s: (b, h)),
        scratch_shapes=[pltpu.VMEM((1, 2 * tq), F32),
                        pltpu.VMEM((DIFF_V + ONES_ROWS, 2 * tq), F32),
                        pltpu.VMEM((t, t), F32), pltpu.VMEM((t, t), F32),
                        pltpu.VMEM((t, 2 * tq), F32), pltpu.VMEM((t, 2 * tq), F32),
                        pltpu.VMEM((1, 2 * tq), F32), pltpu.VMEM((1, 2 * tq), F32)],
    )
    return pl.pallas_call(
        functools.partial(_diff_attn_kernel, t=t, lambda_init=lambda_init),
        grid_spec=grid_spec,
        out_shape=jax.ShapeDtypeStruct((batch * seq, DIFF_HEADS * DIFF_V), BF16),
        compiler_params=_params(2),
        name="diff_attn",
    )(slopes, proj, proj, dvt, vec(lq1), vec(lk1), vec(lq2), vec(lk2), vec(sub_g))


def _merge_kernel(ya_ref, yb_ref, wa_ref, wb_ref, ga_ref, gb_ref, ba_ref, bb_ref, o_ref):
    a = jnp.dot(ya_ref[...], wa_ref[...], preferred_element_type=F32)
    b = jnp.dot(yb_ref[...], wb_ref[...], preferred_element_type=F32)
    g0 = jax.nn.sigmoid(ga_ref[...].astype(F32) + ba_ref[...])
    g1 = jax.nn.sigmoid(gb_ref[...].astype(F32) + bb_ref[...])
    o_ref[...] = (g0 * a + g1 * b).astype(o_ref.dtype)


def _merge(y_mla, y_diff, w_mla, w_diff, proj, b_gate, tm, tn):
    m = y_mla.shape[0]
    d = D_MODEL
    g0_blk = COL_GATE // tn
    g1_blk = (COL_GATE + d) // tn
    nb = d // tn
    bg = b_gate.reshape(1, 2 * d).astype(F32)
    return pl.pallas_call(
        _merge_kernel,
        grid=(m // tm, nb),
        in_specs=[pl.BlockSpec((tm, d), lambda i, j: (i, 0)),
                  pl.BlockSpec((tm, d), lambda i, j: (i, 0)),
                  pl.BlockSpec((d, tn), lambda i, j: (0, j)),
                  pl.BlockSpec((d, tn), lambda i, j: (0, j)),
                  pl.BlockSpec((tm, tn), lambda i, j: (i, g0_blk + j)),
                  pl.BlockSpec((tm, tn), lambda i, j: (i, g1_blk + j)),
                  pl.BlockSpec((1, tn), lambda i, j: (0, j)),
                  pl.BlockSpec((1, tn), lambda i, j: (0, nb + j))],
        out_specs=pl.BlockSpec((tm, tn), lambda i, j: (i, j)),
        out_shape=jax.ShapeDtypeStruct((m, d), BF16),
        compiler_params=_params(2),
        name="gated_merge",
    )(y_mla, y_diff, w_mla, w_diff, proj, proj, bg, bg)


def _residual_matmul_kernel(a_ref, w_ref, x_ref, o_ref):
    o_ref[...] = x_ref[...] + jnp.dot(a_ref[...], w_ref[...], preferred_element_type=F32)


def _residual_matmul(a, w, x, tm, tn):
    m, k = a.shape
    n = w.shape[1]
    return pl.pallas_call(
        _residual_matmul_kernel,
        grid=(m // tm, n // tn),
        in_specs=[pl.BlockSpec((tm, k), lambda i, j: (i, 0)),
                  pl.BlockSpec((k, tn), lambda i, j: (0, j)),
                  pl.BlockSpec((tm, tn), lambda i, j: (i, j))],
        out_specs=pl.BlockSpec((tm, tn), lambda i, j: (i, j)),
        out_shape=jax.ShapeDtypeStruct((m, n), F32),
        compiler_params=_params(2),
        name="out_proj_residual",
    )(a, w, x)


def _ffn_up_kernel(x_ref, g_ref, wg_ref, wu_ref, o_ref, h_ref):
    @pl.when(pl.program_id(1) == 0)
    def _():
        h_ref[...] = _rmsnorm_rows(x_ref[...], g_ref[...]).astype(BF16)

    h = h_ref[...]
    gate = jnp.dot(h, wg_ref[...].astype(BF16), preferred_element_type=F32)
    up = jnp.dot(h, wu_ref[...].astype(BF16), preferred_element_type=F32)
    o_ref[...] = (gate * jax.nn.sigmoid(gate) * up).astype(o_ref.dtype)


def _ffn_up(x, g, wg, wu, tm, tn):
    m, d = x.shape
    n = wg.shape[1]
    return pl.pallas_call(
        _ffn_up_kernel,
        grid=(m // tm, n // tn),
        in_specs=[pl.BlockSpec((tm, d), lambda i, j: (i, 0)),
                  pl.BlockSpec((1, d), lambda i, j: (0, 0)),
                  pl.BlockSpec((d, tn), lambda i, j: (0, j)),
                  pl.BlockSpec((d, tn), lambda i, j: (0, j))],
        out_specs=pl.BlockSpec((tm, tn), lambda i, j: (i, j)),
        out_shape=jax.ShapeDtypeStruct((m, n), BF16),
        scratch_shapes=[pltpu.VMEM((tm, d), BF16)],
        compiler_params=_params(2),
        name="ffn_up",
    )(x, g.reshape(1, d), wg, wu)


def _ffn_down_kernel(a_ref, w_ref, x_ref, g_ref, o_ref, *, final_norm):
    y = x_ref[...] + jnp.dot(a_ref[...], w_ref[...], preferred_element_type=F32)
    o_ref[...] = _rmsnorm_rows(y, g_ref[...]) if final_norm else y


def _ffn_down(a, w, x, g, tm, final_norm):
    m, kdim = a.shape
    d = w.shape[1]
    return pl.pallas_call(
        functools.partial(_ffn_down_kernel, final_norm=final_norm),
        grid=(m // tm,),
        in_specs=[pl.BlockSpec((tm, kdim), lambda i: (i, 0)),
                  pl.BlockSpec((kdim, d), lambda i: (0, 0), pipeline_mode=pl.Buffered(1)),
                  pl.BlockSpec((tm, d), lambda i: (i, 0)),
                  pl.BlockSpec((1, d), lambda i: (0, 0))],
        out_specs=pl.BlockSpec((tm, d), lambda i: (i, 0)),
        out_shape=jax.ShapeDtypeStruct((m, d), F32),
        compiler_params=_params(1),
        name="ffn_down_final_norm",
    )(a, w, x, g.reshape(1, d))


def _rotate_half_cols(w):
    half = w.shape[-1] // 2
    return jnp.concatenate([-w[..., half:], w[..., :half]], axis=-1)


def _prep_w_in(w_in):
    d = D_MODEL
    sizes = (MLA_Q_RANK, MLA_KV_RANK, MLA_ROPE, d, d, d, 2 * d)
    offs = [0]
    for sz in sizes:
        offs.append(offs[-1] + sz)
    cq, ckv, kpe, dq, dk, dv, gate = [w_in[:, offs[n]:offs[n + 1]] for n in range(len(sizes))]
    zeros = lambda n: jnp.zeros((d, n), w_in.dtype)
    dq = dq * (LOG2E / math.sqrt(DIFF_HEAD_DIM))
    cols = [cq, kpe, _rotate_half_cols(kpe), zeros(COL_CKV - COL_KPE - 2 * MLA_ROPE),
            ckv, zeros(COL_GATE - COL_CKV - MLA_KV_RANK), gate, dq, dk]
    return jnp.concatenate(cols, axis=1).astype(BF16), dv.T.astype(BF16)


def _prep_w_uq(w_uq):
    w = w_uq.reshape(MLA_Q_RANK, MLA_HEADS, MLA_QK)
    nope, rope = w[..., :MLA_NOPE], w[..., MLA_NOPE:]
    w = jnp.concatenate([nope, rope, _rotate_half_cols(rope)], axis=-1)
    return w.reshape(MLA_Q_RANK, MLA_HEADS * MLA_QPAD).astype(BF16)


def _prep_w_ukv(w_ukv):
    w = w_ukv.reshape(MLA_KV_RANK, MLA_HEADS, MLA_NOPE + MLA_V)
    k_nope = w[..., :MLA_NOPE].reshape(MLA_KV_RANK, MLA_HEADS * MLA_NOPE)
    v = w[..., MLA_NOPE:].reshape(MLA_KV_RANK, MLA_HEADS * MLA_V)
    return k_nope.astype(BF16), v.T.astype(BF16)


def _rope_tables(seq):
    inv = ROPE_THETA ** (-jnp.arange(0, MLA_ROPE, 2, dtype=F32) / MLA_ROPE)
    ang = jnp.arange(seq, dtype=F32)[:, None] * inv[None, :]
    cos, sin = jnp.cos(ang), jnp.sin(ang)
    zeros = jnp.zeros((seq, LANE - MLA_ROPE), F32)
    return (jnp.concatenate([cos, cos, zeros], axis=1), jnp.concatenate([sin, sin, zeros], axis=1))


def _tile(n, pref):
    t = pref
    while n % t:
        t //= 2
    return t


def kernel(x, attn_norm_g, w_in, b_gate, q_norm_g, w_uq, kv_norm_g, w_ukv, lambda_q1, lambda_k1,
           lambda_q2, lambda_k2, diff_norm_g, w_mla_proj, w_diff_proj, w_out, ffn_norm_g,
           w_ffn_gate, w_ffn_up, w_ffn_down, final_norm_g):
    batch, seq, d = x.shape
    depth = w_in.shape[0]
    assert d == D_MODEL and seq % 1024 == 0
    tokens = batch * seq
    xt = x.reshape(tokens, d)
    c1, c2 = _rope_tables(seq)
    slopes = jnp.exp2(-8.0 * jnp.arange(1, DIFF_HEADS + 1, dtype=F32) / DIFF_HEADS) * LOG2E
    tm = _tile(seq, 1024)
    t_attn = 512

    for layer in range(depth):
        lambda_init = 0.8 - 0.6 * math.exp(-0.3 * layer)
        w1, w_dv_t = _prep_w_in(w_in[layer])
        w_kn, w_v_t = _prep_w_ukv(w_ukv[layer])
        proj = _norm_matmul(xt, d, 0, attn_norm_g[layer], w1, tm, 1024)
        dvt = _norm_matmul_nt(xt, d, 0, attn_norm_g[layer], w_dv_t, tm, 1024)
        qm = _q_proj(proj, q_norm_g[layer], _prep_w_uq(w_uq[layer]), c1, c2, seq, tm, 1024)
        kn = _norm_matmul(proj, MLA_KV_RANK, COL_CKV // MLA_KV_RANK, kv_norm_g[layer], w_kn,
                          tm, 1024)
        vt = _norm_matmul_nt(proj, MLA_KV_RANK, COL_CKV // MLA_KV_RANK, kv_norm_g[layer], w_v_t,
                             tm, 1024)
        kr = _k_rope(proj, c1, c2, seq, tm)
        y_mla = _mla_attn(qm, kn, kr, vt, batch, seq, t_attn)
        y_diff = _diff_attn(proj, dvt, slopes, lambda_q1[layer], lambda_k1[layer], lambda_q2[layer],
                            lambda_k2[layer], diff_norm_g[layer], batch, seq, t_attn, lambda_init)
        merged = _merge(y_mla, y_diff, w_mla_proj[layer].astype(BF16),
                        w_diff_proj[layer].astype(BF16), proj, b_gate[layer], tm, 1024)
        xt = _residual_matmul(merged, w_out[layer].astype(BF16), xt, tm, 1024)
        a = _ffn_up(xt, ffn_norm_g[layer], w_ffn_gate[layer], w_ffn_up[layer], tm, 512)
        xt = _ffn_down(a, w_ffn_down[layer].astype(BF16), xt, final_norm_g, 512,
                       final_norm=layer == depth - 1)
    return xt.reshape(batch, seq, d)
```

```python
import functools
import math

import jax
import jax.numpy as jnp
from jax import lax
from jax.experimental import pallas as pl
from jax.experimental.pallas import tpu as pltpu

D_MODEL = 2048
CHUNK = 64
EPS = 1e-6
NEG_INF = -1e30
LOG2E = math.log2(math.e)

MLA_HEADS = 16
MLA_Q_RANK = 768
MLA_KV_RANK = 512
MLA_NOPE = 128
MLA_ROPE = 64
MLA_V = 128
MLA_QK = MLA_NOPE + MLA_ROPE
ROPE_THETA = 10000.0

DIFF_HEADS = 8
DIFF_HEAD_DIM = 128
DIFF_V = 2 * DIFF_HEAD_DIM

LANE = 128
MLA_QPAD = 2 * LANE

COL_CQ = 0
COL_KPE = 768
COL_CKV = 1024
COL_GATE = 2048
COL_DQ = COL_GATE + 2 * D_MODEL
COL_DK = COL_DQ + D_MODEL
PROJ_COLS = COL_DK + D_MODEL

VMEM_LIMIT_BYTES = 56 * 1024 * 1024

BF16 = jnp.bfloat16
F32 = jnp.float32
NT_DIMS = (((1,), (1,)), ((), ()))


def _params(n_axes):
    return pltpu.CompilerParams(dimension_semantics=("arbitrary",) * n_axes,
                                vmem_limit_bytes=VMEM_LIMIT_BYTES)


def _rmsnorm_rows(x, g):
    ms = jnp.mean(x * x, axis=-1, keepdims=True)
    return x * lax.rsqrt(ms + EPS) * g


def _norm_matmul_kernel(x_ref, g_ref, w_ref, o_ref, h_ref):
    @pl.when(pl.program_id(1) == 0)
    def _():
        h_ref[...] = _rmsnorm_rows(x_ref[...].astype(F32), g_ref[...]).astype(BF16)

    o_ref[...] = jnp.dot(h_ref[...], w_ref[...], preferred_element_type=F32).astype(o_ref.dtype)


def _norm_matmul(x, x_cols, x_col_block, g, w, tm, tn):
    m = x.shape[0]
    n = w.shape[1]
    return pl.pallas_call(
        _norm_matmul_kernel,
        grid=(m // tm, n // tn),
        in_specs=[pl.BlockSpec((tm, x_cols), lambda i, j: (i, x_col_block)),
                  pl.BlockSpec((1, x_cols), lambda i, j: (0, 0)),
                  pl.BlockSpec((x_cols, tn), lambda i, j: (0, j))],
        out_specs=pl.BlockSpec((tm, tn), lambda i, j: (i, j)),
        out_shape=jax.ShapeDtypeStruct((m, n), BF16),
        scratch_shapes=[pltpu.VMEM((tm, x_cols), BF16)],
        compiler_params=_params(2),
        name="norm_matmul",
    )(x, g.reshape(1, x_cols), w)


def _norm_matmul_nt_kernel(x_ref, g_ref, wt_ref, o_ref, h_ref):
    @pl.when(pl.program_id(1) == 0)
    def _():
        h_ref[...] = _rmsnorm_rows(x_ref[...].astype(F32), g_ref[...]).astype(BF16)

    o_ref[...] = lax.dot_general(wt_ref[...], h_ref[...], NT_DIMS,
                                 preferred_element_type=F32).astype(o_ref.dtype)


def _norm_matmul_nt(x, x_cols, x_col_block, g, wt, tm, tn):
    m = x.shape[0]
    n = wt.shape[0]
    return pl.pallas_call(
        _norm_matmul_nt_kernel,
        grid=(m // tm, n // tn),
        in_specs=[pl.BlockSpec((tm, x_cols), lambda i, j: (i, x_col_block)),
                  pl.BlockSpec((1, x_cols), lambda i, j: (0, 0)),
                  pl.BlockSpec((tn, x_cols), lambda i, j: (j, 0))],
        out_specs=pl.BlockSpec((tn, tm), lambda i, j: (j, i)),
        out_shape=jax.ShapeDtypeStruct((n, m), BF16),
        scratch_shapes=[pltpu.VMEM((tm, x_cols), BF16)],
        compiler_params=_params(2),
        name="norm_matmul_nt",
    )(x, g.reshape(1, x_cols), wt)


def _rope_block(blk, c1, c2):
    return blk * c1 + pltpu.roll(blk, LANE // 2, 1) * c2


def _q_proj_kernel(x_ref, g_ref, w_ref, c1_ref, c2_ref, o_ref, h_ref, *, heads_per_tile, qscale):
    @pl.when(pl.program_id(1) == 0)
    def _():
        h_ref[...] = _rmsnorm_rows(x_ref[...].astype(F32), g_ref[...]).astype(BF16)

    acc = jnp.dot(h_ref[...], w_ref[...], preferred_element_type=F32)
    c1 = c1_ref[...]
    c2 = c2_ref[...]
    for hh in range(heads_per_tile):
        lo = hh * MLA_QPAD
        o_ref[:, lo:lo + LANE] = (acc[:, lo:lo + LANE] * qscale).astype(o_ref.dtype)
        roped = _rope_block(acc[:, lo + LANE:lo + MLA_QPAD], c1, c2)
        o_ref[:, lo + LANE:lo + MLA_QPAD] = (roped * qscale).astype(o_ref.dtype)


def _q_proj(proj, g, w, c1, c2, seq, tm, tn):
    m = proj.shape[0]
    n = w.shape[1]
    s_tiles = seq // tm
    return pl.pallas_call(
        functools.partial(_q_proj_kernel, heads_per_tile=tn // MLA_QPAD,
                          qscale=LOG2E / math.sqrt(MLA_QK)),
        grid=(m // tm, n // tn),
        in_specs=[pl.BlockSpec((tm, MLA_Q_RANK), lambda i, j: (i, COL_CQ // MLA_Q_RANK)),
                  pl.BlockSpec((1, MLA_Q_RANK), lambda i, j: (0, 0)),
                  pl.BlockSpec((MLA_Q_RANK, tn), lambda i, j: (0, j)),
                  pl.BlockSpec((tm, LANE), lambda i, j: (i % s_tiles, 0)),
                  pl.BlockSpec((tm, LANE), lambda i, j: (i % s_tiles, 0))],
        out_specs=pl.BlockSpec((tm, tn), lambda i, j: (i, j)),
        out_shape=jax.ShapeDtypeStruct((m, n), BF16),
        scratch_shapes=[pltpu.VMEM((tm, MLA_Q_RANK), BF16)],
        compiler_params=_params(2),
        name="mla_q_proj",
    )(proj, g.reshape(1, MLA_Q_RANK), w, c1, c2)


def _k_rope_kernel(x_ref, c1_ref, c2_ref, o_ref):
    o_ref[...] = _rope_block(x_ref[...].astype(F32), c1_ref[...], c2_ref[...]).astype(o_ref.dtype)


def _k_rope(proj, c1, c2, seq, tm):
    m = proj.shape[0]
    s_tiles = seq // tm
    return pl.pallas_call(
        _k_rope_kernel,
        grid=(m // tm,),
        in_specs=[pl.BlockSpec((tm, LANE), lambda i: (i, COL_KPE // LANE)),
                  pl.BlockSpec((tm, LANE), lambda i: (i % s_tiles, 0)),
                  pl.BlockSpec((tm, LANE), lambda i: (i % s_tiles, 0))],
        out_specs=pl.BlockSpec((tm, LANE), lambda i: (i, 0)),
        out_shape=jax.ShapeDtypeStruct((m, LANE), BF16),
        compiler_params=_params(1),
        name="mla_k_rope",
    )(proj, c1, c2)


ONES_ROWS = 16
MLA_ONES_ROWS = 16


def _chunk_mask_t(t):
    shift = int(math.log2(CHUNK))
    kc = lax.shift_right_logical(lax.broadcasted_iota(jnp.int32, (t, t), 0), shift)
    qc = lax.shift_right_logical(lax.broadcasted_iota(jnp.int32, (t, t), 1), shift)
    return kc <= qc


def _softmax_update(s_t, vt, m_sc, acc_sc, cols=slice(None), col_max=None, shift=None,
                    ones_rows=ONES_ROWS):
    if col_max is None:
        col_max = jnp.max(s_t, axis=0, keepdims=True)
    if shift is not None:
        col_max = col_max + shift
    m_prev = m_sc[:, cols]
    m_new = jnp.maximum(m_prev, col_max)
    alpha = jnp.exp2(m_prev - m_new)
    m_sub = m_new if shift is None else m_new - shift
    p_t = jnp.exp2(s_t - m_sub).astype(BF16)
    v_aug = jnp.concatenate([vt, jnp.ones((ones_rows, vt.shape[1]), BF16)], axis=0)
    acc_sc[:, cols] = alpha * acc_sc[:, cols] + jnp.dot(v_aug, p_t, preferred_element_type=F32)
    m_sc[:, cols] = m_new


def _attend_query_tiles(n_tiles, n_diag, stages, bufs, pairs_per_trip=1):
    assert n_diag % 2 == 0
    stages.qk(0, 0, *bufs[0])

    def tile(i, carry):
        stages.init(i)

        def pair(p):
            j = 2 * p
            stages.qk(i, j + 1, *bufs[1])
            stages.pv(i, j, *bufs[0])
            stages.qk(i, j + 2, *bufs[0])
            stages.pv(i, j + 1, *bufs[1])

        def trip(p, carry):
            for u in range(pairs_per_trip):
                pair(pairs_per_trip * p + u)
            return carry

        def single(p, carry):
            pair(p)
            return carry

        n_pairs = (n_diag // 2) * i
        n_trips = lax.div(n_pairs, pairs_per_trip)
        lax.fori_loop(0, n_trips, trip, 0)
        if pairs_per_trip > 1:
            lax.fori_loop(n_trips * pairs_per_trip, n_pairs, single, 0)
        j = n_diag * i
        for d in range(n_diag):
            if d + 1 < n_diag:
                stages.qk_diag(i, j + d + 1, d + 1, *bufs[(d + 1) % 2])
            else:
                stages.qk(jnp.minimum(i + 1, n_tiles - 1), 0, *bufs[0])
            stages.diag(i, j + d, d, *bufs[d % 2])
        stages.finalize(i)
        return carry

    lax.fori_loop(0, n_tiles, tile, 0)


class _Stages:
    def __init__(self, **fns):
        self.__dict__.update(fns)


def _mla_attn_kernel(q_ref, kn_ref, kr_ref, vt_ref, o_ref, m_sc, acc_sc, fix_sc, s0_sc, s1_sc,
                     cm0_sc, cm1_sc, *, t, tq):
    fix_sc[...] = jnp.where(_chunk_mask_t(t), 0.0, NEG_INF)

    def key_rows(j):
        return pl.ds(pl.multiple_of(j * t, t), t)

    def qry_rows(i, lo=0):
        return pl.ds(pl.multiple_of(i * tq + lo, t), tq - lo)

    def keys(j):
        rows = key_rows(j)
        return jnp.concatenate([kn_ref[rows, :], kr_ref[rows, :]], axis=1)

    def init(i):
        m_sc[...] = jnp.full(m_sc.shape, NEG_INF, F32)
        acc_sc[...] = jnp.zeros(acc_sc.shape, F32)

    def qk(i, j, s_sc, cm_sc):
        s_t = lax.dot_general(keys(j), q_ref[qry_rows(i), :], NT_DIMS, preferred_element_type=F32)
        s_sc[...] = s_t
        cm_sc[...] = jnp.max(s_t, axis=0, keepdims=True)

    def qk_diag(i, j, d, s_sc, cm_sc):
        s_sc[:, d * t:] = lax.dot_general(keys(j), q_ref[qry_rows(i, d * t), :], NT_DIMS,
                                          preferred_element_type=F32)

    def pv(i, j, s_sc, cm_sc):
        _softmax_update(s_sc[...], vt_ref[:, key_rows(j)], m_sc, acc_sc, col_max=cm_sc[...],
                        ones_rows=MLA_ONES_ROWS)

    def diag(i, j, d, s_sc, cm_sc):
        lo = d * t
        parts = [s_sc[:, lo:lo + t] + fix_sc[...]]
        if lo + t < tq:
            parts.append(s_sc[:, lo + t:])
        _softmax_update(jnp.concatenate(parts, axis=1), vt_ref[:, key_rows(j)], m_sc, acc_sc,
                        cols=slice(lo, tq), ones_rows=MLA_ONES_ROWS)

    def finalize(i):
        acc = acc_sc[...]
        o_t = acc[:MLA_V] * (1.0 / acc[MLA_V:MLA_V + 1])
        o_ref[qry_rows(i), :] = o_t.T.astype(o_ref.dtype)

    stages = _Stages(init=init, qk=qk, qk_diag=qk_diag, pv=pv, diag=diag, finalize=finalize)
    _attend_query_tiles(q_ref.shape[0] // tq, tq // t, stages,
                        ((s0_sc, cm0_sc), (s1_sc, cm1_sc)), pairs_per_trip=2)


def _mla_attn(qm, kn, kr, vt, batch, seq, t, tq):
    return pl.pallas_call(
        functools.partial(_mla_attn_kernel, t=t, tq=tq),
        grid=(batch, MLA_HEADS),
        in_specs=[pl.BlockSpec((seq, MLA_QPAD), lambda b, h: (b, h)),
                  pl.BlockSpec((seq, LANE), lambda b, h: (b, h)),
                  pl.BlockSpec((seq, LANE), lambda b, h: (b, 0)),
                  pl.BlockSpec((MLA_V, seq), lambda b, h: (h, b))],
        out_specs=pl.BlockSpec((seq, MLA_V), lambda b, h: (b, h)),
        out_shape=jax.ShapeDtypeStruct((batch * seq, MLA_HEADS * MLA_V), BF16),
        scratch_shapes=[pltpu.VMEM((1, tq), F32),
                        pltpu.VMEM((MLA_V + MLA_ONES_ROWS, tq), F32),
                        pltpu.VMEM((t, t), F32),
                        pltpu.VMEM((t, tq), F32), pltpu.VMEM((t, tq), F32),
                        pltpu.VMEM((1, tq), F32), pltpu.VMEM((1, tq), F32)],
        compiler_params=_params(2),
        name="mla_attn",
    )(qm, kn, kr, vt)


def _diff_attn_kernel(slope_ref, q_ref, k_ref, vt_ref, lq1_ref, lk1_ref, lq2_ref, lk2_ref, sg_ref,
                      o_ref, m_sc, acc_sc, bias_sc, fix_sc, s0_sc, s1_sc, cm0_sc, cm1_sc, *, t,
                      lambda_init):
    d = DIFF_HEAD_DIM
    tq = 2 * t
    slope = slope_ref[pl.program_id(1)]

    key_idx = lax.broadcasted_iota(jnp.int32, (t, t), 0)
    qry_idx = lax.broadcasted_iota(jnp.int32, (t, t), 1)
    bias_sc[...] = key_idx.astype(F32) * slope
    fix = jnp.maximum(key_idx - qry_idx, 0).astype(F32) * (-2.0 * slope)
    fix_sc[...] = jnp.where(_chunk_mask_t(t), fix, NEG_INF)
    lam = (jnp.exp(jnp.sum(lq1_ref[...] * lk1_ref[...], axis=1, keepdims=True))
           - jnp.exp(jnp.sum(lq2_ref[...] * lk2_ref[...], axis=1, keepdims=True))
           + lambda_init)

    def key_rows(j):
        return pl.ds(pl.multiple_of(j * t, t), t)

    def qry_rows(i, lo=0):
        return pl.ds(pl.multiple_of(i * tq + lo, t), tq - lo)

    def scores(i, j, m, lo=0):
        lanes = slice(m * d, (m + 1) * d)
        s = lax.dot_general(k_ref[key_rows(j), lanes], q_ref[qry_rows(i, lo), lanes], NT_DIMS,
                            preferred_element_type=F32)
        return s + jnp.concatenate([bias_sc[...]] * (s.shape[1] // t), axis=1)

    def block_shift(i, j):
        return lax.convert_element_type(j * t - i * tq, F32) * slope

    def init(i):
        m_sc[...] = jnp.full(m_sc.shape, NEG_INF, F32)
        acc_sc[...] = jnp.zeros(acc_sc.shape, F32)

    def qk(i, j, s_sc, cm_sc):
        s_t = jnp.concatenate([scores(i, j, 0), scores(i, j, 1)], axis=1)
        s_sc[...] = s_t
        cm_sc[...] = jnp.max(s_t, axis=0, keepdims=True)

    def qk_diag(i, j, d, s_sc, cm_sc):
        for m in range(2):
            s_sc[:, m * tq + d * t:(m + 1) * tq] = scores(i, j, m, d * t)

    def pv(i, j, s_sc, cm_sc):
        _softmax_update(s_sc[...], vt_ref[:, key_rows(j)], m_sc, acc_sc, col_max=cm_sc[...],
                        shift=block_shift(i, j))

    def diag(i, j, d, s_sc, cm_sc):
        if d == 0:
            fix = fix_sc[...]
            s_t = jnp.concatenate([s_sc[:, :t] + fix, s_sc[:, t:tq], s_sc[:, tq:tq + t] + fix,
                                   s_sc[:, tq + t:]], axis=1)
            _softmax_update(s_t, vt_ref[:, key_rows(j)], m_sc, acc_sc, shift=block_shift(i, j))
            return
        for m in range(2):
            lo = m * tq + d * t
            parts = [s_sc[:, lo:lo + t] + fix_sc[...]]
            if lo + t < (m + 1) * tq:
                parts.append(s_sc[:, lo + t:(m + 1) * tq])
            _softmax_update(jnp.concatenate(parts, axis=1), vt_ref[:, key_rows(j)], m_sc, acc_sc,
                            cols=slice(lo, (m + 1) * tq), shift=block_shift(i, j))

    def finalize(i):
        acc = acc_sc[...]
        o_t = acc[:DIFF_V] * (1.0 / acc[DIFF_V:DIFF_V + 1])
        o = (o_t[:, :tq] - lam * o_t[:, tq:]).T
        y = _rmsnorm_rows(o, sg_ref[...]) * (1.0 - lambda_init)
        o_ref[qry_rows(i), :] = y.astype(o_ref.dtype)

    stages = _Stages(init=init, qk=qk, qk_diag=qk_diag, pv=pv, diag=diag, finalize=finalize)
    _attend_query_tiles(q_ref.shape[0] // tq, tq // t, stages,
                        ((s0_sc, cm0_sc), (s1_sc, cm1_sc)), pairs_per_trip=2)


def _diff_attn(proj, dvt, slopes, lq1, lk1, lq2, lk2, sub_g, batch, seq, t, lambda_init):
    tq = 2 * t
    vec = lambda a: a.reshape(1, -1).astype(F32)
    small = lambda n: pl.BlockSpec((1, n), lambda b, h, s: (0, 0))
    grid_spec = pltpu.PrefetchScalarGridSpec(
        num_scalar_prefetch=1,
        grid=(batch, DIFF_HEADS),
        in_specs=[pl.BlockSpec((seq, DIFF_V), lambda b, h, s: (b, COL_DQ // DIFF_V + h)),
                  pl.BlockSpec((seq, DIFF_V), lambda b, h, s: (b, COL_DK // DIFF_V + h)),
                  pl.BlockSpec((DIFF_V, seq), lambda b, h, s: (h, b)),
                  small(DIFF_HEAD_DIM), small(DIFF_HEAD_DIM), small(DIFF_HEAD_DIM),
                  small(DIFF_HEAD_DIM), small(DIFF_V)],
        out_specs=pl.BlockSpec((seq, DIFF_V), lambda b, h, s: (b, h)),
        scratch_shapes=[pltpu.VMEM((1, 2 * tq), F32),
                        pltpu.VMEM((DIFF_V + ONES_ROWS, 2 * tq), F32),
                        pltpu.VMEM((t, t), F32), pltpu.VMEM((t, t), F32),
                        pltpu.VMEM((t, 2 * tq), F32), pltpu.VMEM((t, 2 * tq), F32),
                        pltpu.VMEM((1, 2 * tq), F32), pltpu.VMEM((1, 2 * tq), F32)],
    )
    return pl.pallas_call(
        functools.partial(_diff_attn_kernel, t=t, lambda_init=lambda_init),
        grid_spec=grid_spec,
        out_shape=jax.ShapeDtypeStruct((batch * seq, DIFF_HEADS * DIFF_V), BF16),
        compiler_params=_params(2),
        name="diff_attn",
    )(slopes, proj, proj, dvt, vec(lq1), vec(lk1), vec(lq2), vec(lk2), vec(sub_g))


def _merge_kernel(ya_ref, yb_ref, wa_ref, wb_ref, ga_ref, gb_ref, ba_ref, bb_ref, o_ref):
    a = jnp.dot(ya_ref[...], wa_ref[...], preferred_element_type=F32)
    b = jnp.dot(yb_ref[...], wb_ref[...], preferred_element_type=F32)
    g0 = jax.nn.sigmoid(ga_ref[...].astype(F32) + ba_ref[...])
    g1 = jax.nn.sigmoid(gb_ref[...].astype(F32) + bb_ref[...])
    o_ref[...] = (g0 * a + g1 * b).astype(o_ref.dtype)


def _merge(y_mla, y_diff, w_mla, w_diff, proj, b_gate, tm, tn):
    m = y_mla.shape[0]
    d = D_MODEL
    g0_blk = COL_GATE // tn
    g1_blk = (COL_GATE + d) // tn
    nb = d // tn
    bg = b_gate.reshape(1, 2 * d).astype(F32)
    return pl.pallas_call(
        _merge_kernel,
        grid=(m // tm, nb),
        in_specs=[pl.BlockSpec((tm, d), lambda i, j: (i, 0)),
                  pl.BlockSpec((tm, d), lambda i, j: (i, 0)),
                  pl.BlockSpec((d, tn), lambda i, j: (0, j)),
                  pl.BlockSpec((d, tn), lambda i, j: (0, j)),
                  pl.BlockSpec((tm, tn), lambda i, j: (i, g0_blk + j)),
                  pl.BlockSpec((tm, tn), lambda i, j: (i, g1_blk + j)),
                  pl.BlockSpec((1, tn), lambda i, j: (0, j)),
                  pl.BlockSpec((1, tn), lambda i, j: (0, nb + j))],
        out_specs=pl.BlockSpec((tm, tn), lambda i, j: (i, j)),
        out_shape=jax.ShapeDtypeStruct((m, d), BF16),
        compiler_params=_params(2),
        name="gated_merge",
    )(y_mla, y_diff, w_mla, w_diff, proj, proj, bg, bg)


def _residual_matmul_kernel(a_ref, w_ref, x_ref, o_ref):
    o_ref[...] = x_ref[...] + jnp.dot(a_ref[...], w_ref[...], preferred_element_type=F32)


def _residual_matmul(a, w, x, tm, tn):
    m, k = a.shape
    n = w.shape[1]
    return pl.pallas_call(
        _residual_matmul_kernel,
        grid=(m // tm, n // tn),
        in_specs=[pl.BlockSpec((tm, k), lambda i, j: (i, 0)),
                  pl.BlockSpec((k, tn), lambda i, j: (0, j)),
                  pl.BlockSpec((tm, tn), lambda i, j: (i, j))],
        out_specs=pl.BlockSpec((tm, tn), lambda i, j: (i, j)),
        out_shape=jax.ShapeDtypeStruct((m, n), F32),
        compiler_params=_params(2),
        name="out_proj_residual",
    )(a, w, x)


def _ffn_up_kernel(x_ref, g_ref, wg_ref, wu_ref, o_ref, h_ref):
    @pl.when(pl.program_id(1) == 0)
    def _():
        h_ref[...] = _rmsnorm_rows(x_ref[...], g_ref[...]).astype(BF16)

    h = h_ref[...]
    gate = jnp.dot(h, wg_ref[...].astype(BF16), preferred_element_type=F32)
    up = jnp.dot(h, wu_ref[...].astype(BF16), preferred_element_type=F32)
    o_ref[...] = (gate * jax.nn.sigmoid(gate) * up).astype(o_ref.dtype)


def _ffn_up(x, g, wg, wu, tm, tn):
    m, d = x.shape
    n = wg.shape[1]
    return pl.pallas_call(
        _ffn_up_kernel,
        grid=(m // tm, n // tn),
        in_specs=[pl.BlockSpec((tm, d), lambda i, j: (i, 0)),
                  pl.BlockSpec((1, d), lambda i, j: (0, 0)),
                  pl.BlockSpec((d, tn), lambda i, j: (0, j)),
                  pl.BlockSpec((d, tn), lambda i, j: (0, j))],
        out_specs=pl.BlockSpec((tm, tn), lambda i, j: (i, j)),
        out_shape=jax.ShapeDtypeStruct((m, n), BF16),
        scratch_shapes=[pltpu.VMEM((tm, d), BF16)],
        compiler_params=_params(2),
        name="ffn_up",
    )(x, g.reshape(1, d), wg, wu)


def _ffn_down_kernel(a_ref, w_ref, x_ref, g_ref, o_ref, *, final_norm):
    y = x_ref[...] + jnp.dot(a_ref[...], w_ref[...], preferred_element_type=F32)
    o_ref[...] = _rmsnorm_rows(y, g_ref[...]) if final_norm else y


def _ffn_down(a, w, x, g, tm, final_norm):
    m, kdim = a.shape
    d = w.shape[1]
    return pl.pallas_call(
        functools.partial(_ffn_down_kernel, final_norm=final_norm),
        grid=(m // tm,),
        in_specs=[pl.BlockSpec((tm, kdim), lambda i: (i, 0)),
                  pl.BlockSpec((kdim, d), lambda i: (0, 0), pipeline_mode=pl.Buffered(1)),
                  pl.BlockSpec((tm, d), lambda i: (i, 0)),
                  pl.BlockSpec((1, d), lambda i: (0, 0))],
        out_specs=pl.BlockSpec((tm, d), lambda i: (i, 0)),
        out_shape=jax.ShapeDtypeStruct((m, d), F32),
        compiler_params=_params(1),
        name="ffn_down_final_norm",
    )(a, w, x, g.reshape(1, d))


W_IN_SIZES = (MLA_Q_RANK, MLA_KV_RANK, MLA_ROPE, D_MODEL, D_MODEL, D_MODEL, 2 * D_MODEL)
W_IN_OFFS = tuple(sum(W_IN_SIZES[:n]) for n in range(len(W_IN_SIZES) + 1))


def _w_in_layout_kernel(w_ref, o_ref, dvt_ref):
    o_cq, o_ckv, o_kpe, o_dq, o_dk, o_dv, o_gate, o_end = W_IN_OFFS
    w = w_ref[...]
    rows = w.shape[0]
    kpe = w[:, o_kpe:o_dq]
    half = MLA_ROPE // 2
    kpe_rot = jnp.concatenate([-kpe[:, half:], kpe[:, :half]], axis=1)
    zeros = lambda n: jnp.zeros((rows, n), BF16)
    o_ref[:, COL_CQ:COL_CQ + MLA_Q_RANK] = w[:, o_cq:o_ckv].astype(BF16)
    o_ref[:, COL_KPE:COL_KPE + 2 * MLA_ROPE] = jnp.concatenate([kpe, kpe_rot], axis=1).astype(BF16)
    o_ref[:, COL_KPE + 2 * MLA_ROPE:COL_CKV] = zeros(COL_CKV - COL_KPE - 2 * MLA_ROPE)
    o_ref[:, COL_CKV:COL_CKV + MLA_KV_RANK] = w[:, o_ckv:o_kpe].astype(BF16)
    o_ref[:, COL_CKV + MLA_KV_RANK:COL_GATE] = zeros(COL_GATE - COL_CKV - MLA_KV_RANK)
    o_ref[:, COL_GATE:COL_DQ] = w[:, o_gate:o_end].astype(BF16)
    o_ref[:, COL_DQ:COL_DK] = (w[:, o_dq:o_dk] * (LOG2E / math.sqrt(DIFF_HEAD_DIM))).astype(BF16)
    o_ref[:, COL_DK:PROJ_COLS] = w[:, o_dk:o_dv].astype(BF16)
    dvt_ref[...] = w[:, o_dv:o_gate].T.astype(BF16)


def _w_in_layout(w_in, tr):
    d, n = w_in.shape
    return pl.pallas_call(
        _w_in_layout_kernel,
        grid=(d // tr,),
        in_specs=[pl.BlockSpec((tr, n), lambda i: (i, 0))],
        out_specs=[pl.BlockSpec((tr, PROJ_COLS), lambda i: (i, 0)),
                   pl.BlockSpec((D_MODEL, tr), lambda i: (0, i))],
        out_shape=[jax.ShapeDtypeStruct((d, PROJ_COLS), BF16),
                   jax.ShapeDtypeStruct((D_MODEL, d), BF16)],
        compiler_params=_params(1),
        name="w_in_layout",
    )(w_in)


def _rotate_half_cols(w):
    half = w.shape[-1] // 2
    return jnp.concatenate([-w[..., half:], w[..., :half]], axis=-1)


def _prep_w_uq(w_uq):
    w = w_uq.reshape(MLA_Q_RANK, MLA_HEADS, MLA_QK)
    nope, rope = w[..., :MLA_NOPE], w[..., MLA_NOPE:]
    w = jnp.concatenate([nope, rope, _rotate_half_cols(rope)], axis=-1)
    return w.reshape(MLA_Q_RANK, MLA_HEADS * MLA_QPAD).astype(BF16)


def _prep_w_ukv(w_ukv):
    w = w_ukv.reshape(MLA_KV_RANK, MLA_HEADS, MLA_NOPE + MLA_V)
    k_nope = w[..., :MLA_NOPE].reshape(MLA_KV_RANK, MLA_HEADS * MLA_NOPE)
    v = w[..., MLA_NOPE:].reshape(MLA_KV_RANK, MLA_HEADS * MLA_V)
    return k_nope.astype(BF16), v.T.astype(BF16)


def _rope_tables(seq):
    inv = ROPE_THETA ** (-jnp.arange(0, MLA_ROPE, 2, dtype=F32) / MLA_ROPE)
    ang = jnp.arange(seq, dtype=F32)[:, None] * inv[None, :]
    cos, sin = jnp.cos(ang), jnp.sin(ang)
    zeros = jnp.zeros((seq, LANE - MLA_ROPE), F32)
    return (jnp.concatenate([cos, cos, zeros], axis=1), jnp.concatenate([sin, sin, zeros], axis=1))


def _tile(n, pref):
    t = pref
    while n % t:
        t //= 2
    return t


def kernel(x, attn_norm_g, w_in, b_gate, q_norm_g, w_uq, kv_norm_g, w_ukv, lambda_q1, lambda_k1,
           lambda_q2, lambda_k2, diff_norm_g, w_mla_proj, w_diff_proj, w_out, ffn_norm_g,
           w_ffn_gate, w_ffn_up, w_ffn_down, final_norm_g):
    batch, seq, d = x.shape
    depth = w_in.shape[0]
    t_attn = 512
    assert d == D_MODEL and seq % (4 * t_attn) == 0
    tokens = batch * seq
    xt = x.reshape(tokens, d)
    c1, c2 = _rope_tables(seq)
    slopes = jnp.exp2(-8.0 * jnp.arange(1, DIFF_HEADS + 1, dtype=F32) / DIFF_HEADS) * LOG2E
    tm = _tile(seq, 1024)

    for layer in range(depth):
        lambda_init = 0.8 - 0.6 * math.exp(-0.3 * layer)
        w1, w_dv_t = _w_in_layout(w_in[layer], 256)
        w_kn, w_v_t = _prep_w_ukv(w_ukv[layer])
        proj = _norm_matmul(xt, d, 0, attn_norm_g[layer], w1, tm, 1024)
        dvt = _norm_matmul_nt(xt, d, 0, attn_norm_g[layer], w_dv_t, tm, 1024)
        qm = _q_proj(proj, q_norm_g[layer], _prep_w_uq(w_uq[layer]), c1, c2, seq, tm, 1024)
        kn = _norm_matmul(proj, MLA_KV_RANK, COL_CKV // MLA_KV_RANK, kv_norm_g[layer], w_kn,
                          tm, 1024)
        vt = _norm_matmul_nt(proj, MLA_KV_RANK, COL_CKV // MLA_KV_RANK, kv_norm_g[layer], w_v_t,
                             tm, 1024)
        kr = _k_rope(proj, c1, c2, seq, tm)
        y_mla = _mla_attn(qm, kn, kr, vt, batch, seq, t_attn, 4 * t_attn)
        y_diff = _diff_attn(proj, dvt, slopes, lambda_q1[layer], lambda_k1[layer], lambda_q2[layer],
                            lambda_k2[layer], diff_norm_g[layer], batch, seq, t_attn, lambda_init)
        merged = _merge(y_mla, y_diff, w_mla_proj[layer].astype(BF16),
                        w_diff_proj[layer].astype(BF16), proj, b_gate[layer], tm, 1024)
        xt = _residual_matmul(merged, w_out[layer].astype(BF16), xt, tm, 1024)
        a = _ffn_up(xt, ffn_norm_g[layer], w_ffn_gate[layer], w_ffn_up[layer], tm, 512)
        xt = _ffn_down(a, w_ffn_down[layer].astype(BF16), xt, final_norm_g, 512,
                       final_norm=layer == depth - 1)
    return xt.reshape(batch, seq, d)
```

```python
import functools
import math

import jax
import jax.numpy as jnp
from jax import lax
from jax.experimental import pallas as pl
from jax.experimental.pallas import tpu as pltpu

D_MODEL = 2048
CHUNK = 64
EPS = 1e-6
NEG_INF = -1e30
LOG2E = math.log2(math.e)

MLA_HEADS = 16
MLA_Q_RANK = 768
MLA_KV_RANK = 512
MLA_NOPE = 128
MLA_ROPE = 64
MLA_V = 128
MLA_QK = MLA_NOPE + MLA_ROPE
ROPE_THETA = 10000.0

DIFF_HEADS = 8
DIFF_HEAD_DIM = 128
DIFF_V = 2 * DIFF_HEAD_DIM

LANE = 128
MLA_QPAD = 2 * LANE

COL_CQ = 0
COL_KPE = 768
COL_CKV = 1024
COL_GATE = 2048
COL_DQ = COL_GATE + 2 * D_MODEL
COL_DK = COL_DQ + D_MODEL
PROJ_COLS = COL_DK + D_MODEL

VMEM_LIMIT_BYTES = 56 * 1024 * 1024

BF16 = jnp.bfloat16
F32 = jnp.float32
NT_DIMS = (((1,), (1,)), ((), ()))


def _params(n_axes):
    return pltpu.CompilerParams(dimension_semantics=("arbitrary",) * n_axes,
                                vmem_limit_bytes=VMEM_LIMIT_BYTES)


def _rmsnorm_rows(x, g):
    ms = jnp.mean(x * x, axis=-1, keepdims=True)
    return x * lax.rsqrt(ms + EPS) * g


def _norm_matmul_kernel(x_ref, g_ref, w_ref, o_ref, h_ref):
    @pl.when(pl.program_id(1) == 0)
    def _():
        h_ref[...] = _rmsnorm_rows(x_ref[...].astype(F32), g_ref[...]).astype(BF16)

    o_ref[...] = jnp.dot(h_ref[...], w_ref[...], preferred_element_type=F32).astype(o_ref.dtype)


def _norm_matmul(x, x_cols, x_col_block, g, w, tm, tn):
    m = x.shape[0]
    n = w.shape[1]
    return pl.pallas_call(
        _norm_matmul_kernel,
        grid=(m // tm, n // tn),
        in_specs=[pl.BlockSpec((tm, x_cols), lambda i, j: (i, x_col_block)),
                  pl.BlockSpec((1, x_cols), lambda i, j: (0, 0)),
                  pl.BlockSpec((x_cols, tn), lambda i, j: (0, j))],
        out_specs=pl.BlockSpec((tm, tn), lambda i, j: (i, j)),
        out_shape=jax.ShapeDtypeStruct((m, n), BF16),
        scratch_shapes=[pltpu.VMEM((tm, x_cols), BF16)],
        compiler_params=_params(2),
        name="norm_matmul",
    )(x, g.reshape(1, x_cols), w)


def _norm_matmul_wt_kernel(x_ref, g_ref, wt_ref, o_ref, h_ref):
    @pl.when(pl.program_id(1) == 0)
    def _():
        h_ref[...] = _rmsnorm_rows(x_ref[...].astype(F32), g_ref[...]).astype(BF16)

    o_ref[...] = lax.dot_general(h_ref[...], wt_ref[...], NT_DIMS,
                                 preferred_element_type=F32).astype(o_ref.dtype)


def _norm_matmul_wt(x, g, wt, tm, tn):
    m, k = x.shape
    n = wt.shape[0]
    return pl.pallas_call(
        _norm_matmul_wt_kernel,
        grid=(m // tm, n // tn),
        in_specs=[pl.BlockSpec((tm, k), lambda i, j: (i, 0)),
                  pl.BlockSpec((1, k), lambda i, j: (0, 0)),
                  pl.BlockSpec((tn, k), lambda i, j: (j, 0))],
        out_specs=pl.BlockSpec((tm, tn), lambda i, j: (i, j)),
        out_shape=jax.ShapeDtypeStruct((m, n), BF16),
        scratch_shapes=[pltpu.VMEM((tm, k), BF16)],
        compiler_params=_params(2),
        name="in_proj",
    )(x, g.reshape(1, k), wt)


def _norm_matmul_nt_kernel(x_ref, g_ref, wt_ref, o_ref, h_ref):
    @pl.when(pl.program_id(1) == 0)
    def _():
        h_ref[...] = _rmsnorm_rows(x_ref[...].astype(F32), g_ref[...]).astype(BF16)

    o_ref[...] = lax.dot_general(wt_ref[...], h_ref[...], NT_DIMS,
                                 preferred_element_type=F32).astype(o_ref.dtype)


def _norm_matmul_nt(x, x_cols, x_col_block, g, wt, tm, tn):
    m = x.shape[0]
    n = wt.shape[0]
    return pl.pallas_call(
        _norm_matmul_nt_kernel,
        grid=(m // tm, n // tn),
        in_specs=[pl.BlockSpec((tm, x_cols), lambda i, j: (i, x_col_block)),
                  pl.BlockSpec((1, x_cols), lambda i, j: (0, 0)),
                  pl.BlockSpec((tn, x_cols), lambda i, j: (j, 0))],
        out_specs=pl.BlockSpec((tn, tm), lambda i, j: (j, i)),
        out_shape=jax.ShapeDtypeStruct((n, m), BF16),
        scratch_shapes=[pltpu.VMEM((tm, x_cols), BF16)],
        compiler_params=_params(2),
        name="norm_matmul_nt",
    )(x, g.reshape(1, x_cols), wt)


def _rope_block(blk, c1, c2):
    return blk * c1 + pltpu.roll(blk, LANE // 2, 1) * c2


def _q_proj_kernel(x_ref, g_ref, w_ref, c1_ref, c2_ref, o_ref, h_ref, *, heads_per_tile, qscale):
    @pl.when(pl.program_id(1) == 0)
    def _():
        h_ref[...] = _rmsnorm_rows(x_ref[...].astype(F32), g_ref[...]).astype(BF16)

    acc = jnp.dot(h_ref[...], w_ref[...], preferred_element_type=F32)
    c1 = c1_ref[...]
    c2 = c2_ref[...]
    for hh in range(heads_per_tile):
        lo = hh * MLA_QPAD
        o_ref[:, lo:lo + LANE] = (acc[:, lo:lo + LANE] * qscale).astype(o_ref.dtype)
        roped = _rope_block(acc[:, lo + LANE:lo + MLA_QPAD], c1, c2)
        o_ref[:, lo + LANE:lo + MLA_QPAD] = (roped * qscale).astype(o_ref.dtype)


def _q_proj(proj, g, w, c1, c2, seq, tm, tn):
    m = proj.shape[0]
    n = w.shape[1]
    s_tiles = seq // tm
    return pl.pallas_call(
        functools.partial(_q_proj_kernel, heads_per_tile=tn // MLA_QPAD,
                          qscale=LOG2E / math.sqrt(MLA_QK)),
        grid=(m // tm, n // tn),
        in_specs=[pl.BlockSpec((tm, MLA_Q_RANK), lambda i, j: (i, COL_CQ // MLA_Q_RANK)),
                  pl.BlockSpec((1, MLA_Q_RANK), lambda i, j: (0, 0)),
                  pl.BlockSpec((MLA_Q_RANK, tn), lambda i, j: (0, j)),
                  pl.BlockSpec((tm, LANE), lambda i, j: (i % s_tiles, 0)),
                  pl.BlockSpec((tm, LANE), lambda i, j: (i % s_tiles, 0))],
        out_specs=pl.BlockSpec((tm, tn), lambda i, j: (i, j)),
        out_shape=jax.ShapeDtypeStruct((m, n), BF16),
        scratch_shapes=[pltpu.VMEM((tm, MLA_Q_RANK), BF16)],
        compiler_params=_params(2),
        name="mla_q_proj",
    )(proj, g.reshape(1, MLA_Q_RANK), w, c1, c2)


def _k_rope_kernel(x_ref, c1_ref, c2_ref, o_ref):
    o_ref[...] = _rope_block(x_ref[...].astype(F32), c1_ref[...], c2_ref[...]).astype(o_ref.dtype)


def _k_rope(proj, c1, c2, seq, tm):
    m = proj.shape[0]
    s_tiles = seq // tm
    return pl.pallas_call(
        _k_rope_kernel,
        grid=(m // tm,),
        in_specs=[pl.BlockSpec((tm, LANE), lambda i: (i, COL_KPE // LANE)),
                  pl.BlockSpec((tm, LANE), lambda i: (i % s_tiles, 0)),
                  pl.BlockSpec((tm, LANE), lambda i: (i % s_tiles, 0))],
        out_specs=pl.BlockSpec((tm, LANE), lambda i: (i, 0)),
        out_shape=jax.ShapeDtypeStruct((m, LANE), BF16),
        compiler_params=_params(1),
        name="mla_k_rope",
    )(proj, c1, c2)


ONES_ROWS = 16
MLA_ONES_ROWS = 16


def _chunk_mask_t(t):
    shift = int(math.log2(CHUNK))
    kc = lax.shift_right_logical(lax.broadcasted_iota(jnp.int32, (t, t), 0), shift)
    qc = lax.shift_right_logical(lax.broadcasted_iota(jnp.int32, (t, t), 1), shift)
    return kc <= qc


def _softmax_update(s_t, vt, m_sc, acc_sc, cols=slice(None), col_max=None, shift=None,
                    ones_rows=ONES_ROWS):
    if col_max is None:
        col_max = jnp.max(s_t, axis=0, keepdims=True)
    if shift is not None:
        col_max = col_max + shift
    m_prev = m_sc[:, cols]
    m_new = jnp.maximum(m_prev, col_max)
    alpha = jnp.exp2(m_prev - m_new)
    m_sub = m_new if shift is None else m_new - shift
    p_t = jnp.exp2(s_t - m_sub).astype(BF16)
    v_aug = jnp.concatenate([vt, jnp.ones((ones_rows, vt.shape[1]), BF16)], axis=0)
    acc_sc[:, cols] = alpha * acc_sc[:, cols] + jnp.dot(v_aug, p_t, preferred_element_type=F32)
    m_sc[:, cols] = m_new


def _attend_query_tiles(n_tiles, n_diag, stages, bufs, pairs_per_trip=1):
    assert n_diag % 2 == 0
    stages.qk(0, 0, *bufs[0])

    def tile(i, carry):
        stages.init(i)

        def pair(p):
            j = 2 * p
            stages.qk(i, j + 1, *bufs[1])
            stages.pv(i, j, *bufs[0])
            stages.qk(i, j + 2, *bufs[0])
            stages.pv(i, j + 1, *bufs[1])

        def trip(p, carry):
            for u in range(pairs_per_trip):
                pair(pairs_per_trip * p + u)
            return carry

        def single(p, carry):
            pair(p)
            return carry

        n_pairs = (n_diag // 2) * i
        n_trips = lax.div(n_pairs, pairs_per_trip)
        lax.fori_loop(0, n_trips, trip, 0)
        if pairs_per_trip > 1:
            lax.fori_loop(n_trips * pairs_per_trip, n_pairs, single, 0)
        j = n_diag * i
        for d in range(n_diag):
            if d + 1 < n_diag:
                stages.qk_diag(i, j + d + 1, d + 1, *bufs[(d + 1) % 2])
            else:
                stages.qk(jnp.minimum(i + 1, n_tiles - 1), 0, *bufs[0])
            stages.diag(i, j + d, d, *bufs[d % 2])
        stages.finalize(i)
        return carry

    lax.fori_loop(0, n_tiles, tile, 0)


class _Stages:
    def __init__(self, **fns):
        self.__dict__.update(fns)


def _mla_attn_kernel(q_ref, kn_ref, kr_ref, vt_ref, o_ref, m_sc, acc_sc, fix_sc, s0_sc, s1_sc,
                     cm0_sc, cm1_sc, *, t, tq):
    fix_sc[...] = jnp.where(_chunk_mask_t(t), 0.0, NEG_INF)

    def key_rows(j):
        return pl.ds(pl.multiple_of(j * t, t), t)

    def qry_rows(i, lo=0):
        return pl.ds(pl.multiple_of(i * tq + lo, t), tq - lo)

    def keys(j):
        rows = key_rows(j)
        return jnp.concatenate([kn_ref[rows, :], kr_ref[rows, :]], axis=1)

    def init(i):
        m_sc[...] = jnp.full(m_sc.shape, NEG_INF, F32)
        acc_sc[...] = jnp.zeros(acc_sc.shape, F32)

    def qk(i, j, s_sc, cm_sc):
        s_t = lax.dot_general(keys(j), q_ref[qry_rows(i), :], NT_DIMS, preferred_element_type=F32)
        s_sc[...] = s_t
        cm_sc[...] = jnp.max(s_t, axis=0, keepdims=True)

    def qk_diag(i, j, d, s_sc, cm_sc):
        s_sc[:, d * t:] = lax.dot_general(keys(j), q_ref[qry_rows(i, d * t), :], NT_DIMS,
                                          preferred_element_type=F32)

    def pv(i, j, s_sc, cm_sc):
        _softmax_update(s_sc[...], vt_ref[:, key_rows(j)], m_sc, acc_sc, col_max=cm_sc[...],
                        ones_rows=MLA_ONES_ROWS)

    def diag(i, j, d, s_sc, cm_sc):
        lo = d * t
        parts = [s_sc[:, lo:lo + t] + fix_sc[...]]
        if lo + t < tq:
            parts.append(s_sc[:, lo + t:])
        _softmax_update(jnp.concatenate(parts, axis=1), vt_ref[:, key_rows(j)], m_sc, acc_sc,
                        cols=slice(lo, tq), ones_rows=MLA_ONES_ROWS)

    def finalize(i):
        acc = acc_sc[...]
        o_t = acc[:MLA_V] * (1.0 / acc[MLA_V:MLA_V + 1])
        o_ref[qry_rows(i), :] = o_t.T.astype(o_ref.dtype)

    stages = _Stages(init=init, qk=qk, qk_diag=qk_diag, pv=pv, diag=diag, finalize=finalize)
    _attend_query_tiles(q_ref.shape[0] // tq, tq // t, stages,
                        ((s0_sc, cm0_sc), (s1_sc, cm1_sc)), pairs_per_trip=2)


def _mla_attn(qm, kn, kr, vt, batch, seq, t, tq):
    return pl.pallas_call(
        functools.partial(_mla_attn_kernel, t=t, tq=tq),
        grid=(batch, MLA_HEADS),
        in_specs=[pl.BlockSpec((seq, MLA_QPAD), lambda b, h: (b, h)),
                  pl.BlockSpec((seq, LANE), lambda b, h: (b, h)),
                  pl.BlockSpec((seq, LANE), lambda b, h: (b, 0)),
                  pl.BlockSpec((MLA_V, seq), lambda b, h: (h, b))],
        out_specs=pl.BlockSpec((seq, MLA_V), lambda b, h: (b, h)),
        out_shape=jax.ShapeDtypeStruct((batch * seq, MLA_HEADS * MLA_V), BF16),
        scratch_shapes=[pltpu.VMEM((1, tq), F32),
                        pltpu.VMEM((MLA_V + MLA_ONES_ROWS, tq), F32),
                        pltpu.VMEM((t, t), F32),
                        pltpu.VMEM((t, tq), F32), pltpu.VMEM((t, tq), F32),
                        pltpu.VMEM((1, tq), F32), pltpu.VMEM((1, tq), F32)],
        compiler_params=_params(2),
        name="mla_attn",
    )(qm, kn, kr, vt)


def _diff_attn_kernel(slope_ref, q_ref, k_ref, vt_ref, lq1_ref, lk1_ref, lq2_ref, lk2_ref, sg_ref,
                      o_ref, m_sc, acc_sc, bias_sc, fix_sc, s0_sc, s1_sc, cm0_sc, cm1_sc, *, t,
                      lambda_init):
    d = DIFF_HEAD_DIM
    tq = 2 * t
    slope = slope_ref[pl.program_id(1)]

    key_idx = lax.broadcasted_iota(jnp.int32, (t, t), 0)
    qry_idx = lax.broadcasted_iota(jnp.int32, (t, t), 1)
    bias_sc[...] = key_idx.astype(F32) * slope
    fix = jnp.maximum(key_idx - qry_idx, 0).astype(F32) * (-2.0 * slope)
    fix_sc[...] = jnp.where(_chunk_mask_t(t), fix, NEG_INF)
    lam = (jnp.exp(jnp.sum(lq1_ref[...] * lk1_ref[...], axis=1, keepdims=True))
           - jnp.exp(jnp.sum(lq2_ref[...] * lk2_ref[...], axis=1, keepdims=True))
           + lambda_init)

    def key_rows(j):
        return pl.ds(pl.multiple_of(j * t, t), t)

    def qry_rows(i, lo=0):
        return pl.ds(pl.multiple_of(i * tq + lo, t), tq - lo)

    def scores(i, j, m, lo=0):
        lanes = slice(m * d, (m + 1) * d)
        s = lax.dot_general(k_ref[key_rows(j), lanes], q_ref[qry_rows(i, lo), lanes], NT_DIMS,
                            preferred_element_type=F32)
        return s + jnp.concatenate([bias_sc[...]] * (s.shape[1] // t), axis=1)

    def block_shift(i, j):
        return lax.convert_element_type(j * t - i * tq, F32) * slope

    def init(i):
        m_sc[...] = jnp.full(m_sc.shape, NEG_INF, F32)
        acc_sc[...] = jnp.zeros(acc_sc.shape, F32)

    def qk(i, j, s_sc, cm_sc):
        s_t = jnp.concatenate([scores(i, j, 0), scores(i, j, 1)], axis=1)
        s_sc[...] = s_t
        cm_sc[...] = jnp.max(s_t, axis=0, keepdims=True)

    def qk_diag(i, j, d, s_sc, cm_sc):
        for m in range(2):
            s_sc[:, m * tq + d * t:(m + 1) * tq] = scores(i, j, m, d * t)

    def pv(i, j, s_sc, cm_sc):
        _softmax_update(s_sc[...], vt_ref[:, key_rows(j)], m_sc, acc_sc, col_max=cm_sc[...],
                        shift=block_shift(i, j))

    def diag(i, j, d, s_sc, cm_sc):
        if d == 0:
            fix = fix_sc[...]
            s_t = jnp.concatenate([s_sc[:, :t] + fix, s_sc[:, t:tq], s_sc[:, tq:tq + t] + fix,
                                   s_sc[:, tq + t:]], axis=1)
            _softmax_update(s_t, vt_ref[:, key_rows(j)], m_sc, acc_sc, shift=block_shift(i, j))
            return
        for m in range(2):
            lo = m * tq + d * t
            parts = [s_sc[:, lo:lo + t] + fix_sc[...]]
            if lo + t < (m + 1) * tq:
                parts.append(s_sc[:, lo + t:(m + 1) * tq])
            _softmax_update(jnp.concatenate(parts, axis=1), vt_ref[:, key_rows(j)], m_sc, acc_sc,
                            cols=slice(lo, (m + 1) * tq), shift=block_shift(i, j))

    def finalize(i):
        acc = acc_sc[...]
        o_t = acc[:DIFF_V] * (1.0 / acc[DIFF_V:DIFF_V + 1])
        o = (o_t[:, :tq] - lam * o_t[:, tq:]).T
        y = _rmsnorm_rows(o, sg_ref[...]) * (1.0 - lambda_init)
        o_ref[qry_rows(i), :] = y.astype(o_ref.dtype)

    stages = _Stages(init=init, qk=qk, qk_diag=qk_diag, pv=pv, diag=diag, finalize=finalize)
    _attend_query_tiles(q_ref.shape[0] // tq, tq // t, stages,
                        ((s0_sc, cm0_sc), (s1_sc, cm1_sc)), pairs_per_trip=2)


def _diff_attn(proj, dvt, slopes, lq1, lk1, lq2, lk2, sub_g, batch, seq, t, lambda_init):
    tq = 2 * t
    vec = lambda a: a.reshape(1, -1).astype(F32)
    small = lambda n: pl.BlockSpec((1, n), lambda b, h, s: (0, 0))
    grid_spec = pltpu.PrefetchScalarGridSpec(
        num_scalar_prefetch=1,
        grid=(batch, DIFF_HEADS),
        in_specs=[pl.BlockSpec((seq, DIFF_V), lambda b, h, s: (b, COL_DQ // DIFF_V + h)),
                  pl.BlockSpec((seq, DIFF_V), lambda b, h, s: (b, COL_DK // DIFF_V + h)),
                  pl.BlockSpec((DIFF_V, seq), lambda b, h, s: (h, b)),
                  small(DIFF_HEAD_DIM), small(DIFF_HEAD_DIM), small(DIFF_HEAD_DIM),
                  small(DIFF_HEAD_DIM), small(DIFF_V)],
        out_specs=pl.BlockSpec((seq, DIFF_V), lambda b, h, s: (b, h)),
        scratch_shapes=[pltpu.VMEM((1, 2 * tq), F32),
                        pltpu.VMEM((DIFF_V + ONES_ROWS, 2 * tq), F32),
                        pltpu.VMEM((t, t), F32), pltpu.VMEM((t, t), F32),
                        pltpu.VMEM((t, 2 * tq), F32), pltpu.VMEM((t, 2 * tq), F32),
                        pltpu.VMEM((1, 2 * tq), F32), pltpu.VMEM((1, 2 * tq), F32)],
    )
    return pl.pallas_call(
        functools.partial(_diff_attn_kernel, t=t, lambda_init=lambda_init),
        grid_spec=grid_spec,
        out_shape=jax.ShapeDtypeStruct((batch * seq, DIFF_HEADS * DIFF_V), BF16),
        compiler_params=_params(2),
        name="diff_attn",
    )(slopes, proj, proj, dvt, vec(lq1), vec(lk1), vec(lq2), vec(lk2), vec(sub_g))


def _merge_kernel(ya_ref, yb_ref, wa_ref, wb_ref, ga_ref, gb_ref, ba_ref, bb_ref, o_ref):
    a = jnp.dot(ya_ref[...], wa_ref[...], preferred_element_type=F32)
    b = jnp.dot(yb_ref[...], wb_ref[...], preferred_element_type=F32)
    g0 = jax.nn.sigmoid(ga_ref[...].astype(F32) + ba_ref[...])
    g1 = jax.nn.sigmoid(gb_ref[...].astype(F32) + bb_ref[...])
    o_ref[...] = (g0 * a + g1 * b).astype(o_ref.dtype)


def _merge(y_mla, y_diff, w_mla, w_diff, proj, b_gate, tm, tn):
    m = y_mla.shape[0]
    d = D_MODEL
    g0_blk = COL_GATE // tn
    g1_blk = (COL_GATE + d) // tn
    nb = d // tn
    bg = b_gate.reshape(1, 2 * d).astype(F32)
    return pl.pallas_call(
        _merge_kernel,
        grid=(m // tm, nb),
        in_specs=[pl.BlockSpec((tm, d), lambda i, j: (i, 0)),
                  pl.BlockSpec((tm, d), lambda i, j: (i, 0)),
                  pl.BlockSpec((d, tn), lambda i, j: (0, j)),
                  pl.BlockSpec((d, tn), lambda i, j: (0, j)),
                  pl.BlockSpec((tm, tn), lambda i, j: (i, g0_blk + j)),
                  pl.BlockSpec((tm, tn), lambda i, j: (i, g1_blk + j)),
                  pl.BlockSpec((1, tn), lambda i, j: (0, j)),
                  pl.BlockSpec((1, tn), lambda i, j: (0, nb + j))],
        out_specs=pl.BlockSpec((tm, tn), lambda i, j: (i, j)),
        out_shape=jax.ShapeDtypeStruct((m, d), BF16),
        compiler_params=_params(2),
        name="gated_merge",
    )(y_mla, y_diff, w_mla, w_diff, proj, proj, bg, bg)


def _residual_matmul_kernel(a_ref, w_ref, x_ref, o_ref):
    o_ref[...] = x_ref[...] + jnp.dot(a_ref[...], w_ref[...], preferred_element_type=F32)


def _residual_matmul(a, w, x, tm, tn):
    m, k = a.shape
    n = w.shape[1]
    return pl.pallas_call(
        _residual_matmul_kernel,
        grid=(m // tm, n // tn),
        in_specs=[pl.BlockSpec((tm, k), lambda i, j: (i, 0)),
                  pl.BlockSpec((k, tn), lambda i, j: (0, j)),
                  pl.BlockSpec((tm, tn), lambda i, j: (i, j))],
        out_specs=pl.BlockSpec((tm, tn), lambda i, j: (i, j)),
        out_shape=jax.ShapeDtypeStruct((m, n), F32),
        compiler_params=_params(2),
        name="out_proj_residual",
    )(a, w, x)


def _ffn_up_kernel(x_ref, g_ref, wg_ref, wu_ref, o_ref, h_ref):
    @pl.when(pl.program_id(1) == 0)
    def _():
        h_ref[...] = _rmsnorm_rows(x_ref[...], g_ref[...]).astype(BF16)

    h = h_ref[...]
    gate = jnp.dot(h, wg_ref[...].astype(BF16), preferred_element_type=F32)
    up = jnp.dot(h, wu_ref[...].astype(BF16), preferred_element_type=F32)
    o_ref[...] = (gate * jax.nn.sigmoid(gate) * up).astype(o_ref.dtype)


def _ffn_up(x, g, wg, wu, tm, tn):
    m, d = x.shape
    n = wg.shape[1]
    return pl.pallas_call(
        _ffn_up_kernel,
        grid=(m // tm, n // tn),
        in_specs=[pl.BlockSpec((tm, d), lambda i, j: (i, 0)),
                  pl.BlockSpec((1, d), lambda i, j: (0, 0)),
                  pl.BlockSpec((d, tn), lambda i, j: (0, j)),
                  pl.BlockSpec((d, tn), lambda i, j: (0, j))],
        out_specs=pl.BlockSpec((tm, tn), lambda i, j: (i, j)),
        out_shape=jax.ShapeDtypeStruct((m, n), BF16),
        scratch_shapes=[pltpu.VMEM((tm, d), BF16)],
        compiler_params=_params(2),
        name="ffn_up",
    )(x, g.reshape(1, d), wg, wu)


def _ffn_down_kernel(a_ref, w_ref, x_ref, g_ref, o_ref, *, final_norm):
    y = x_ref[...] + jnp.dot(a_ref[...], w_ref[...], preferred_element_type=F32)
    o_ref[...] = _rmsnorm_rows(y, g_ref[...]) if final_norm else y


def _ffn_down(a, w, x, g, tm, final_norm):
    m, kdim = a.shape
    d = w.shape[1]
    return pl.pallas_call(
        functools.partial(_ffn_down_kernel, final_norm=final_norm),
        grid=(m // tm,),
        in_specs=[pl.BlockSpec((tm, kdim), lambda i: (i, 0)),
                  pl.BlockSpec((kdim, d), lambda i: (0, 0), pipeline_mode=pl.Buffered(1)),
                  pl.BlockSpec((tm, d), lambda i: (i, 0)),
                  pl.BlockSpec((1, d), lambda i: (0, 0))],
        out_specs=pl.BlockSpec((tm, d), lambda i: (i, 0)),
        out_shape=jax.ShapeDtypeStruct((m, d), F32),
        compiler_params=_params(1),
        name="ffn_down_final_norm",
    )(a, w, x, g.reshape(1, d))


W_IN_SIZES = (MLA_Q_RANK, MLA_KV_RANK, MLA_ROPE, D_MODEL, D_MODEL, D_MODEL, 2 * D_MODEL)
W_IN_OFFS = tuple(sum(W_IN_SIZES[:n]) for n in range(len(W_IN_SIZES) + 1))


def _w_in_layout_kernel(wt_ref, o_ref, dvt_ref):
    o_cq, o_ckv, o_kpe, o_dq, o_dk, o_dv, o_gate, o_end = W_IN_OFFS
    half = MLA_ROPE // 2
    cols = o_ref.shape[1]

    def cast(lo, hi):
        return wt_ref[lo:hi, :].astype(BF16)

    o_ref[COL_CQ:COL_CQ + MLA_Q_RANK, :] = cast(o_cq, o_ckv)
    o_ref[COL_KPE:COL_KPE + MLA_ROPE, :] = cast(o_kpe, o_dq)
    o_ref[COL_KPE + MLA_ROPE:COL_KPE + MLA_ROPE + half, :] = (-wt_ref[o_kpe + half:o_dq, :]).astype(BF16)
    o_ref[COL_KPE + MLA_ROPE + half:COL_KPE + 2 * MLA_ROPE, :] = cast(o_kpe, o_kpe + half)
    o_ref[COL_KPE + 2 * MLA_ROPE:COL_CKV, :] = jnp.zeros((COL_CKV - COL_KPE - 2 * MLA_ROPE, cols), BF16)
    o_ref[COL_CKV:COL_CKV + MLA_KV_RANK, :] = cast(o_ckv, o_kpe)
    o_ref[COL_CKV + MLA_KV_RANK:COL_GATE, :] = jnp.zeros((COL_GATE - COL_CKV - MLA_KV_RANK, cols), BF16)
    o_ref[COL_GATE:COL_DQ, :] = cast(o_gate, o_end)
    o_ref[COL_DQ:COL_DK, :] = (wt_ref[o_dq:o_dk, :] * (LOG2E / math.sqrt(DIFF_HEAD_DIM))).astype(BF16)
    o_ref[COL_DK:PROJ_COLS, :] = cast(o_dk, o_dv)
    dvt_ref[...] = cast(o_dv, o_gate)


def _w_in_layout(w_in_t, tk):
    n, k = w_in_t.shape
    return pl.pallas_call(
        _w_in_layout_kernel,
        grid=(k // tk,),
        in_specs=[pl.BlockSpec((n, tk), lambda i: (0, i))],
        out_specs=[pl.BlockSpec((PROJ_COLS, tk), lambda i: (0, i)),
                   pl.BlockSpec((D_MODEL, tk), lambda i: (0, i))],
        out_shape=[jax.ShapeDtypeStruct((PROJ_COLS, k), BF16),
                   jax.ShapeDtypeStruct((D_MODEL, k), BF16)],
        compiler_params=_params(1),
        name="w_in_layout",
    )(w_in_t)


def _rotate_half_cols(w):
    half = w.shape[-1] // 2
    return jnp.concatenate([-w[..., half:], w[..., :half]], axis=-1)


def _prep_w_uq(w_uq):
    w = w_uq.reshape(MLA_Q_RANK, MLA_HEADS, MLA_QK)
    nope, rope = w[..., :MLA_NOPE], w[..., MLA_NOPE:]
    w = jnp.concatenate([nope, rope, _rotate_half_cols(rope)], axis=-1)
    return w.reshape(MLA_Q_RANK, MLA_HEADS * MLA_QPAD).astype(BF16)


def _prep_w_ukv(w_ukv):
    w = w_ukv.reshape(MLA_KV_RANK, MLA_HEADS, MLA_NOPE + MLA_V)
    k_nope = w[..., :MLA_NOPE].reshape(MLA_KV_RANK, MLA_HEADS * MLA_NOPE)
    v = w[..., MLA_NOPE:].reshape(MLA_KV_RANK, MLA_HEADS * MLA_V)
    return k_nope.astype(BF16), v.T.astype(BF16)


def _rope_tables(seq):
    inv = ROPE_THETA ** (-jnp.arange(0, MLA_ROPE, 2, dtype=F32) / MLA_ROPE)
    ang = jnp.arange(seq, dtype=F32)[:, None] * inv[None, :]
    cos, sin = jnp.cos(ang), jnp.sin(ang)
    zeros = jnp.zeros((seq, LANE - MLA_ROPE), F32)
    return (jnp.concatenate([cos, cos, zeros], axis=1), jnp.concatenate([sin, sin, zeros], axis=1))


def _tile(n, pref):
    t = pref
    while n % t:
        t //= 2
    return t


def kernel(x, attn_norm_g, w_in, b_gate, q_norm_g, w_uq, kv_norm_g, w_ukv, lambda_q1, lambda_k1,
           lambda_q2, lambda_k2, diff_norm_g, w_mla_proj, w_diff_proj, w_out, ffn_norm_g,
           w_ffn_gate, w_ffn_up, w_ffn_down, final_norm_g):
    batch, seq, d = x.shape
    depth = w_in.shape[0]
    t_attn = 512
    assert d == D_MODEL and seq % (4 * t_attn) == 0
    tokens = batch * seq
    xt = x.reshape(tokens, d)
    c1, c2 = _rope_tables(seq)
    slopes = jnp.exp2(-8.0 * jnp.arange(1, DIFF_HEADS + 1, dtype=F32) / DIFF_HEADS) * LOG2E
    tm = _tile(seq, 1024)

    for layer in range(depth):
        lambda_init = 0.8 - 0.6 * math.exp(-0.3 * layer)
        w1_t, w_dv_t = _w_in_layout(w_in[layer].T, 256)
        w_kn, w_v_t = _prep_w_ukv(w_ukv[layer])
        proj = _norm_matmul_wt(xt, attn_norm_g[layer], w1_t, tm, 1024)
        dvt = _norm_matmul_nt(xt, d, 0, attn_norm_g[layer], w_dv_t, tm, 1024)
        qm = _q_proj(proj, q_norm_g[layer], _prep_w_uq(w_uq[layer]), c1, c2, seq, tm, 1024)
        kn = _norm_matmul(proj, MLA_KV_RANK, COL_CKV // MLA_KV_RANK, kv_norm_g[layer], w_kn,
                          tm, 1024)
        vt = _norm_matmul_nt(proj, MLA_KV_RANK, COL_CKV // MLA_KV_RANK, kv_norm_g[layer], w_v_t,
                             tm, 1024)
        kr = _k_rope(proj, c1, c2, seq, tm)
        y_mla = _mla_attn(qm, kn, kr, vt, batch, seq, t_attn, 4 * t_attn)
        y_diff = _diff_attn(proj, dvt, slopes, lambda_q1[layer], lambda_k1[layer], lambda_q2[layer],
                            lambda_k2[layer], diff_norm_g[layer], batch, seq, t_attn, lambda_init)
        merged = _merge(y_mla, y_diff, w_mla_proj[layer].astype(BF16),
                        w_diff_proj[layer].astype(BF16), proj, b_gate[layer], tm, 1024)
        xt = _residual_matmul(merged, w_out[layer].astype(BF16), xt, tm, 1024)
        a = _ffn_up(xt, ffn_norm_g[layer], w_ffn_gate[layer], w_ffn_up[layer], tm, 512)
        xt = _ffn_down(a, w_ffn_down[layer].astype(BF16), xt, final_norm_g, 512,
                       final_norm=layer == depth - 1)
    return xt.reshape(batch, seq, d)
```

```python
import functools
import math

import jax
import jax.numpy as jnp
from jax import lax
from jax.experimental import pallas as pl
from jax.experimental.pallas import tpu as pltpu

D_MODEL = 2048
CHUNK = 64
EPS = 1e-6
NEG_INF = -1e30
LOG2E = math.log2(math.e)

MLA_HEADS = 16
MLA_Q_RANK = 768
MLA_KV_RANK = 512
MLA_NOPE = 128
MLA_ROPE = 64
MLA_V = 128
MLA_QK = MLA_NOPE + MLA_ROPE
ROPE_THETA = 10000.0

DIFF_HEADS = 8
DIFF_HEAD_DIM = 128
DIFF_V = 2 * DIFF_HEAD_DIM

LANE = 128
MLA_QPAD = 2 * LANE

COL_CQ = 0
COL_KPE = 768
COL_CKV = 1024
COL_GATE = 2048
COL_DQ = COL_GATE + 2 * D_MODEL
COL_DK = COL_DQ + D_MODEL
PROJ_COLS = COL_DK + D_MODEL

VMEM_LIMIT_BYTES = 56 * 1024 * 1024

BF16 = jnp.bfloat16
F32 = jnp.float32
NT_DIMS = (((1,), (1,)), ((), ()))


def _params(n_axes):
    return pltpu.CompilerParams(dimension_semantics=("arbitrary",) * n_axes,
                                vmem_limit_bytes=VMEM_LIMIT_BYTES)


def _rmsnorm_rows(x, g):
    ms = jnp.mean(x * x, axis=-1, keepdims=True)
    return x * lax.rsqrt(ms + EPS) * g


def _norm_matmul_kernel(x_ref, g_ref, w_ref, o_ref, h_ref):
    @pl.when(pl.program_id(1) == 0)
    def _():
        h_ref[...] = _rmsnorm_rows(x_ref[...].astype(F32), g_ref[...]).astype(BF16)

    o_ref[...] = jnp.dot(h_ref[...], w_ref[...], preferred_element_type=F32).astype(o_ref.dtype)


def _norm_matmul(x, x_cols, x_col_block, g, w, tm, tn):
    m = x.shape[0]
    n = w.shape[1]
    return pl.pallas_call(
        _norm_matmul_kernel,
        grid=(m // tm, n // tn),
        in_specs=[pl.BlockSpec((tm, x_cols), lambda i, j: (i, x_col_block)),
                  pl.BlockSpec((1, x_cols), lambda i, j: (0, 0)),
                  pl.BlockSpec((x_cols, tn), lambda i, j: (0, j))],
        out_specs=pl.BlockSpec((tm, tn), lambda i, j: (i, j)),
        out_shape=jax.ShapeDtypeStruct((m, n), BF16),
        scratch_shapes=[pltpu.VMEM((tm, x_cols), BF16)],
        compiler_params=_params(2),
        name="norm_matmul",
    )(x, g.reshape(1, x_cols), w)


def _norm_matmul_wt_kernel(x_ref, g_ref, wt_ref, o_ref, h_ref):
    @pl.when(pl.program_id(1) == 0)
    def _():
        h_ref[...] = _rmsnorm_rows(x_ref[...].astype(F32), g_ref[...]).astype(BF16)

    o_ref[...] = lax.dot_general(h_ref[...], wt_ref[...], NT_DIMS,
                                 preferred_element_type=F32).astype(o_ref.dtype)


def _norm_matmul_wt(x, g, wt, tm, tn):
    m, k = x.shape
    n = wt.shape[0]
    return pl.pallas_call(
        _norm_matmul_wt_kernel,
        grid=(m // tm, n // tn),
        in_specs=[pl.BlockSpec((tm, k), lambda i, j: (i, 0)),
                  pl.BlockSpec((1, k), lambda i, j: (0, 0)),
                  pl.BlockSpec((tn, k), lambda i, j: (j, 0))],
        out_specs=pl.BlockSpec((tm, tn), lambda i, j: (i, j)),
        out_shape=jax.ShapeDtypeStruct((m, n), BF16),
        scratch_shapes=[pltpu.VMEM((tm, k), BF16)],
        compiler_params=_params(2),
        name="in_proj",
    )(x, g.reshape(1, k), wt)


def _norm_matmul_nt_kernel(x_ref, g_ref, wt_ref, o_ref, h_ref):
    @pl.when(pl.program_id(1) == 0)
    def _():
        h_ref[...] = _rmsnorm_rows(x_ref[...].astype(F32), g_ref[...]).astype(BF16)

    o_ref[...] = lax.dot_general(wt_ref[...], h_ref[...], NT_DIMS,
                                 preferred_element_type=F32).astype(o_ref.dtype)


def _norm_matmul_nt(x, x_cols, x_col_block, g, wt, tm, tn):
    m = x.shape[0]
    n = wt.shape[0]
    return pl.pallas_call(
        _norm_matmul_nt_kernel,
        grid=(m // tm, n // tn),
        in_specs=[pl.BlockSpec((tm, x_cols), lambda i, j: (i, x_col_block)),
                  pl.BlockSpec((1, x_cols), lambda i, j: (0, 0)),
                  pl.BlockSpec((tn, x_cols), lambda i, j: (j, 0))],
        out_specs=pl.BlockSpec((tn, tm), lambda i, j: (j, i)),
        out_shape=jax.ShapeDtypeStruct((n, m), BF16),
        scratch_shapes=[pltpu.VMEM((tm, x_cols), BF16)],
        compiler_params=_params(2),
        name="norm_matmul_nt",
    )(x, g.reshape(1, x_cols), wt)


def _rope_block(blk, c1, c2):
    return blk * c1 + pltpu.roll(blk, LANE // 2, 1) * c2


def _q_proj_kernel(x_ref, g_ref, w_ref, c1_ref, c2_ref, o_ref, h_ref, *, heads_per_tile, qscale):
    @pl.when(pl.program_id(1) == 0)
    def _():
        h_ref[...] = _rmsnorm_rows(x_ref[...].astype(F32), g_ref[...]).astype(BF16)

    acc = jnp.dot(h_ref[...], w_ref[...], preferred_element_type=F32)
    c1 = c1_ref[...]
    c2 = c2_ref[...]
    for hh in range(heads_per_tile):
        lo = hh * MLA_QPAD
        o_ref[:, lo:lo + LANE] = (acc[:, lo:lo + LANE] * qscale).astype(o_ref.dtype)
        roped = _rope_block(acc[:, lo + LANE:lo + MLA_QPAD], c1, c2)
        o_ref[:, lo + LANE:lo + MLA_QPAD] = (roped * qscale).astype(o_ref.dtype)


def _q_proj(proj, g, w, c1, c2, seq, tm, tn):
    m = proj.shape[0]
    n = w.shape[1]
    s_tiles = seq // tm
    return pl.pallas_call(
        functools.partial(_q_proj_kernel, heads_per_tile=tn // MLA_QPAD,
                          qscale=LOG2E / math.sqrt(MLA_QK)),
        grid=(m // tm, n // tn),
        in_specs=[pl.BlockSpec((tm, MLA_Q_RANK), lambda i, j: (i, COL_CQ // MLA_Q_RANK)),
                  pl.BlockSpec((1, MLA_Q_RANK), lambda i, j: (0, 0)),
                  pl.BlockSpec((MLA_Q_RANK, tn), lambda i, j: (0, j)),
                  pl.BlockSpec((tm, LANE), lambda i, j: (i % s_tiles, 0)),
                  pl.BlockSpec((tm, LANE), lambda i, j: (i % s_tiles, 0))],
        out_specs=pl.BlockSpec((tm, tn), lambda i, j: (i, j)),
        out_shape=jax.ShapeDtypeStruct((m, n), BF16),
        scratch_shapes=[pltpu.VMEM((tm, MLA_Q_RANK), BF16)],
        compiler_params=_params(2),
        name="mla_q_proj",
    )(proj, g.reshape(1, MLA_Q_RANK), w, c1, c2)


def _k_rope_kernel(x_ref, c1_ref, c2_ref, o_ref):
    o_ref[...] = _rope_block(x_ref[...].astype(F32), c1_ref[...], c2_ref[...]).astype(o_ref.dtype)


def _k_rope(proj, c1, c2, seq, tm):
    m = proj.shape[0]
    s_tiles = seq // tm
    return pl.pallas_call(
        _k_rope_kernel,
        grid=(m // tm,),
        in_specs=[pl.BlockSpec((tm, LANE), lambda i: (i, COL_KPE // LANE)),
                  pl.BlockSpec((tm, LANE), lambda i: (i % s_tiles, 0)),
                  pl.BlockSpec((tm, LANE), lambda i: (i % s_tiles, 0))],
        out_specs=pl.BlockSpec((tm, LANE), lambda i: (i, 0)),
        out_shape=jax.ShapeDtypeStruct((m, LANE), BF16),
        compiler_params=_params(1),
        name="mla_k_rope",
    )(proj, c1, c2)


ONES_ROWS = 16
MLA_ONES_ROWS = 16


def _chunk_mask_t(t):
    shift = int(math.log2(CHUNK))
    kc = lax.shift_right_logical(lax.broadcasted_iota(jnp.int32, (t, t), 0), shift)
    qc = lax.shift_right_logical(lax.broadcasted_iota(jnp.int32, (t, t), 1), shift)
    return kc <= qc


def _softmax_update(s_t, vt, m_sc, acc_sc, cols=slice(None), col_max=None, shift=None,
                    ones_rows=ONES_ROWS):
    if col_max is None:
        col_max = jnp.max(s_t, axis=0, keepdims=True)
    if shift is not None:
        col_max = col_max + shift
    m_prev = m_sc[:, cols]
    m_new = jnp.maximum(m_prev, col_max)
    alpha = jnp.exp2(m_prev - m_new)
    m_sub = m_new if shift is None else m_new - shift
    p_t = jnp.exp2(s_t - m_sub).astype(BF16)
    v_aug = jnp.concatenate([vt, jnp.ones((ones_rows, vt.shape[1]), BF16)], axis=0)
    acc_sc[:, cols] = alpha * acc_sc[:, cols] + jnp.dot(v_aug, p_t, preferred_element_type=F32)
    m_sc[:, cols] = m_new


def _attend_query_tiles(n_tiles, n_diag, stages, bufs, pairs_per_trip=1):
    assert n_diag % 2 == 0
    stages.qk(0, 0, *bufs[0])

    def tile(i, carry):
        stages.init(i)

        def pair(p):
            j = 2 * p
            stages.qk(i, j + 1, *bufs[1])
            stages.pv(i, j, *bufs[0])
            stages.qk(i, j + 2, *bufs[0])
            stages.pv(i, j + 1, *bufs[1])

        def trip(p, carry):
            for u in range(pairs_per_trip):
                pair(pairs_per_trip * p + u)
            return carry

        def single(p, carry):
            pair(p)
            return carry

        n_pairs = (n_diag // 2) * i
        n_trips = lax.div(n_pairs, pairs_per_trip)
        lax.fori_loop(0, n_trips, trip, 0)
        if pairs_per_trip > 1:
            lax.fori_loop(n_trips * pairs_per_trip, n_pairs, single, 0)
        j = n_diag * i
        for d in range(n_diag):
            if d + 1 < n_diag:
                stages.qk_diag(i, j + d + 1, d + 1, *bufs[(d + 1) % 2])
            else:
                stages.qk(jnp.minimum(i + 1, n_tiles - 1), 0, *bufs[0])
            stages.diag(i, j + d, d, *bufs[d % 2])
        stages.finalize(i)
        return carry

    lax.fori_loop(0, n_tiles, tile, 0)


class _Stages:
    def __init__(self, **fns):
        self.__dict__.update(fns)


def _mla_attn_kernel(q_ref, kn_ref, kr_ref, vt_ref, o_ref, m_sc, acc_sc, fix_sc, s0_sc, s1_sc,
                     cm0_sc, cm1_sc, *, t, tq):
    fix_sc[...] = jnp.where(_chunk_mask_t(t), 0.0, NEG_INF)

    def key_rows(j):
        return pl.ds(pl.multiple_of(j * t, t), t)

    def qry_rows(i, lo=0):
        return pl.ds(pl.multiple_of(i * tq + lo, t), tq - lo)

    def keys(j):
        rows = key_rows(j)
        return jnp.concatenate([kn_ref[rows, :], kr_ref[rows, :]], axis=1)

    def init(i):
        m_sc[...] = jnp.full(m_sc.shape, NEG_INF, F32)
        acc_sc[...] = jnp.zeros(acc_sc.shape, F32)

    def qk(i, j, s_sc, cm_sc):
        s_t = lax.dot_general(keys(j), q_ref[qry_rows(i), :], NT_DIMS, preferred_element_type=F32)
        s_sc[...] = s_t
        cm_sc[...] = jnp.max(s_t, axis=0, keepdims=True)

    def qk_diag(i, j, d, s_sc, cm_sc):
        s_sc[:, d * t:] = lax.dot_general(keys(j), q_ref[qry_rows(i, d * t), :], NT_DIMS,
                                          preferred_element_type=F32)

    def pv(i, j, s_sc, cm_sc):
        _softmax_update(s_sc[...], vt_ref[:, key_rows(j)], m_sc, acc_sc, col_max=cm_sc[...],
                        ones_rows=MLA_ONES_ROWS)

    def diag(i, j, d, s_sc, cm_sc):
        lo = d * t
        parts = [s_sc[:, lo:lo + t] + fix_sc[...]]
        if lo + t < tq:
            parts.append(s_sc[:, lo + t:])
        _softmax_update(jnp.concatenate(parts, axis=1), vt_ref[:, key_rows(j)], m_sc, acc_sc,
                        cols=slice(lo, tq), ones_rows=MLA_ONES_ROWS)

    def finalize(i):
        acc = acc_sc[...]
        o_t = acc[:MLA_V] * (1.0 / acc[MLA_V:MLA_V + 1])
        o_ref[qry_rows(i), :] = o_t.T.astype(o_ref.dtype)

    stages = _Stages(init=init, qk=qk, qk_diag=qk_diag, pv=pv, diag=diag, finalize=finalize)
    _attend_query_tiles(q_ref.shape[0] // tq, tq // t, stages,
                        ((s0_sc, cm0_sc), (s1_sc, cm1_sc)), pairs_per_trip=2)


def _mla_attn(qm, kn, kr, vt, batch, seq, t, tq):
    return pl.pallas_call(
        functools.partial(_mla_attn_kernel, t=t, tq=tq),
        grid=(batch, MLA_HEADS),
        in_specs=[pl.BlockSpec((seq, MLA_QPAD), lambda b, h: (b, h)),
                  pl.BlockSpec((seq, LANE), lambda b, h: (b, h)),
                  pl.BlockSpec((seq, LANE), lambda b, h: (b, 0)),
                  pl.BlockSpec((MLA_V, seq), lambda b, h: (h, b))],
        out_specs=pl.BlockSpec((seq, MLA_V), lambda b, h: (b, h)),
        out_shape=jax.ShapeDtypeStruct((batch * seq, MLA_HEADS * MLA_V), BF16),
        scratch_shapes=[pltpu.VMEM((1, tq), F32),
                        pltpu.VMEM((MLA_V + MLA_ONES_ROWS, tq), F32),
                        pltpu.VMEM((t, t), F32),
                        pltpu.VMEM((t, tq), F32), pltpu.VMEM((t, tq), F32),
                        pltpu.VMEM((1, tq), F32), pltpu.VMEM((1, tq), F32)],
        compiler_params=_params(2),
        name="mla_attn",
    )(qm, kn, kr, vt)


def _diff_attn_kernel(slope_ref, q_ref, k_ref, vt_ref, lq1_ref, lk1_ref, lq2_ref, lk2_ref, sg_ref,
                      o_ref, m_sc, acc_sc, bias_sc, fix_sc, s0_sc, s1_sc, cm0_sc, cm1_sc, *, t,
                      lambda_init):
    d = DIFF_HEAD_DIM
    tq = 2 * t
    slope = slope_ref[pl.program_id(1)]

    key_idx = lax.broadcasted_iota(jnp.int32, (t, t), 0)
    qry_idx = lax.broadcasted_iota(jnp.int32, (t, t), 1)
    bias_sc[...] = key_idx.astype(F32) * slope
    fix = jnp.maximum(key_idx - qry_idx, 0).astype(F32) * (-2.0 * slope)
    fix_sc[...] = jnp.where(_chunk_mask_t(t), fix, NEG_INF)
    lam = (jnp.exp(jnp.sum(lq1_ref[...] * lk1_ref[...], axis=1, keepdims=True))
           - jnp.exp(jnp.sum(lq2_ref[...] * lk2_ref[...], axis=1, keepdims=True))
           + lambda_init)

    def key_rows(j):
        return pl.ds(pl.multiple_of(j * t, t), t)

    def qry_rows(i, lo=0):
        return pl.ds(pl.multiple_of(i * tq + lo, t), tq - lo)

    def scores(i, j, m, lo=0):
        lanes = slice(m * d, (m + 1) * d)
        s = lax.dot_general(k_ref[key_rows(j), lanes], q_ref[qry_rows(i, lo), lanes], NT_DIMS,
                            preferred_element_type=F32)
        return s + jnp.concatenate([bias_sc[...]] * (s.shape[1] // t), axis=1)

    def block_shift(i, j):
        return lax.convert_element_type(j * t - i * tq, F32) * slope

    def init(i):
        m_sc[...] = jnp.full(m_sc.shape, NEG_INF, F32)
        acc_sc[...] = jnp.zeros(acc_sc.shape, F32)

    def qk(i, j, s_sc, cm_sc):
        s_t = jnp.concatenate([scores(i, j, 0), scores(i, j, 1)], axis=1)
        s_sc[...] = s_t
        cm_sc[...] = jnp.max(s_t, axis=0, keepdims=True)

    def qk_diag(i, j, d, s_sc, cm_sc):
        for m in range(2):
            s_sc[:, m * tq + d * t:(m + 1) * tq] = scores(i, j, m, d * t)

    def pv(i, j, s_sc, cm_sc):
        _softmax_update(s_sc[...], vt_ref[:, key_rows(j)], m_sc, acc_sc, col_max=cm_sc[...],
                        shift=block_shift(i, j))

    def diag(i, j, d, s_sc, cm_sc):
        if d == 0:
            fix = fix_sc[...]
            s_t = jnp.concatenate([s_sc[:, :t] + fix, s_sc[:, t:tq], s_sc[:, tq:tq + t] + fix,
                                   s_sc[:, tq + t:]], axis=1)
            _softmax_update(s_t, vt_ref[:, key_rows(j)], m_sc, acc_sc, shift=block_shift(i, j))
            return
        for m in range(2):
            lo = m * tq + d * t
            parts = [s_sc[:, lo:lo + t] + fix_sc[...]]
            if lo + t < (m + 1) * tq:
                parts.append(s_sc[:, lo + t:(m + 1) * tq])
            _softmax_update(jnp.concatenate(parts, axis=1), vt_ref[:, key_rows(j)], m_sc, acc_sc,
                            cols=slice(lo, (m + 1) * tq), shift=block_shift(i, j))

    def finalize(i):
        acc = acc_sc[...]
        o_t = acc[:DIFF_V] * (1.0 / acc[DIFF_V:DIFF_V + 1])
        o = (o_t[:, :tq] - lam * o_t[:, tq:]).T
        y = _rmsnorm_rows(o, sg_ref[...]) * (1.0 - lambda_init)
        o_ref[qry_rows(i), :] = y.astype(o_ref.dtype)

    stages = _Stages(init=init, qk=qk, qk_diag=qk_diag, pv=pv, diag=diag, finalize=finalize)
    _attend_query_tiles(q_ref.shape[0] // tq, tq // t, stages,
                        ((s0_sc, cm0_sc), (s1_sc, cm1_sc)), pairs_per_trip=2)


def _diff_attn(proj, dvt, slopes, lq1, lk1, lq2, lk2, sub_g, batch, seq, t, lambda_init):
    tq = 2 * t
    vec = lambda a: a.reshape(1, -1).astype(F32)
    small = lambda n: pl.BlockSpec((1, n), lambda b, h, s: (0, 0))
    grid_spec = pltpu.PrefetchScalarGridSpec(
        num_scalar_prefetch=1,
        grid=(batch, DIFF_HEADS),
        in_specs=[pl.BlockSpec((seq, DIFF_V), lambda b, h, s: (b, COL_DQ // DIFF_V + h)),
                  pl.BlockSpec((seq, DIFF_V), lambda b, h, s: (b, COL_DK // DIFF_V + h)),
                  pl.BlockSpec((DIFF_V, seq), lambda b, h, s: (h, b)),
                  small(DIFF_HEAD_DIM), small(DIFF_HEAD_DIM), small(DIFF_HEAD_DIM),
                  small(DIFF_HEAD_DIM), small(DIFF_V)],
        out_specs=pl.BlockSpec((seq, DIFF_V), lambda b, h, s: (b, h)),
        scratch_shapes=[pltpu.VMEM((1, 2 * tq), F32),
                        pltpu.VMEM((DIFF_V + ONES_ROWS, 2 * tq), F32),
                        pltpu.VMEM((t, t), F32), pltpu.VMEM((t, t), F32),
                        pltpu.VMEM((t, 2 * tq), F32), pltpu.VMEM((t, 2 * tq), F32),
                        pltpu.VMEM((1, 2 * tq), F32), pltpu.VMEM((1, 2 * tq), F32)],
    )
    return pl.pallas_call(
        functools.partial(_diff_attn_kernel, t=t, lambda_init=lambda_init),
        grid_spec=grid_spec,
        out_shape=jax.ShapeDtypeStruct((batch * seq, DIFF_HEADS * DIFF_V), BF16),
        compiler_params=_params(2),
        name="diff_attn",
    )(slopes, proj, proj, dvt, vec(lq1), vec(lk1), vec(lq2), vec(lk2), vec(sub_g))


def _merge_kernel(ya_ref, yb_ref, wa_ref, wb_ref, ga_ref, gb_ref, ba_ref, bb_ref, o_ref):
    a = jnp.dot(ya_ref[...], wa_ref[...], preferred_element_type=F32)
    b = jnp.dot(yb_ref[...], wb_ref[...], preferred_element_type=F32)
    g0 = jax.nn.sigmoid(ga_ref[...].astype(F32) + ba_ref[...])
    g1 = jax.nn.sigmoid(gb_ref[...].astype(F32) + bb_ref[...])
    o_ref[...] = (g0 * a + g1 * b).astype(o_ref.dtype)


def _merge(y_mla, y_diff, w_mla, w_diff, proj, b_gate, tm, tn):
    m = y_mla.shape[0]
    d = D_MODEL
    g0_blk = COL_GATE // tn
    g1_blk = (COL_GATE + d) // tn
    nb = d // tn
    bg = b_gate.reshape(1, 2 * d).astype(F32)
    return pl.pallas_call(
        _merge_kernel,
        grid=(m // tm, nb),
        in_specs=[pl.BlockSpec((tm, d), lambda i, j: (i, 0)),
                  pl.BlockSpec((tm, d), lambda i, j: (i, 0)),
                  pl.BlockSpec((d, tn), lambda i, j: (0, j)),
                  pl.BlockSpec((d, tn), lambda i, j: (0, j)),
                  pl.BlockSpec((tm, tn), lambda i, j: (i, g0_blk + j)),
                  pl.BlockSpec((tm, tn), lambda i, j: (i, g1_blk + j)),
                  pl.BlockSpec((1, tn), lambda i, j: (0, j)),
                  pl.BlockSpec((1, tn), lambda i, j: (0, nb + j))],
        out_specs=pl.BlockSpec((tm, tn), lambda i, j: (i, j)),
        out_shape=jax.ShapeDtypeStruct((m, d), BF16),
        compiler_params=_params(2),
        name="gated_merge",
    )(y_mla, y_diff, w_mla, w_diff, proj, proj, bg, bg)


def _residual_matmul_kernel(a_ref, w_ref, x_ref, o_ref):
    o_ref[...] = x_ref[...] + jnp.dot(a_ref[...], w_ref[...], preferred_element_type=F32)


def _residual_matmul(a, w, x, tm, tn):
    m, k = a.shape
    n = w.shape[1]
    return pl.pallas_call(
        _residual_matmul_kernel,
        grid=(m // tm, n // tn),
        in_specs=[pl.BlockSpec((tm, k), lambda i, j: (i, 0)),
                  pl.BlockSpec((k, tn), lambda i, j: (0, j)),
                  pl.BlockSpec((tm, tn), lambda i, j: (i, j))],
        out_specs=pl.BlockSpec((tm, tn), lambda i, j: (i, j)),
        out_shape=jax.ShapeDtypeStruct((m, n), F32),
        compiler_params=_params(2),
        name="out_proj_residual",
    )(a, w, x)


def _ffn_up_kernel(x_ref, g_ref, wg_ref, wu_ref, o_ref, h_ref):
    @pl.when(pl.program_id(1) == 0)
    def _():
        h_ref[...] = _rmsnorm_rows(x_ref[...], g_ref[...]).astype(BF16)

    h = h_ref[...]
    gate = jnp.dot(h, wg_ref[...].astype(BF16), preferred_element_type=F32)
    up = jnp.dot(h, wu_ref[...].astype(BF16), preferred_element_type=F32)
    o_ref[...] = (gate * jax.nn.sigmoid(gate) * up).astype(o_ref.dtype)


def _ffn_up(x, g, wg, wu, tm, tn):
    m, d = x.shape
    n = wg.shape[1]
    return pl.pallas_call(
        _ffn_up_kernel,
        grid=(m // tm, n // tn),
        in_specs=[pl.BlockSpec((tm, d), lambda i, j: (i, 0)),
                  pl.BlockSpec((1, d), lambda i, j: (0, 0)),
                  pl.BlockSpec((d, tn), lambda i, j: (0, j)),
                  pl.BlockSpec((d, tn), lambda i, j: (0, j))],
        out_specs=pl.BlockSpec((tm, tn), lambda i, j: (i, j)),
        out_shape=jax.ShapeDtypeStruct((m, n), BF16),
        scratch_shapes=[pltpu.VMEM((tm, d), BF16)],
        compiler_params=_params(2),
        name="ffn_up",
    )(x, g.reshape(1, d), wg, wu)


def _ffn_down_kernel(a_ref, w_ref, x_ref, g_ref, o_ref, *, final_norm):
    y = x_ref[...] + jnp.dot(a_ref[...], w_ref[...], preferred_element_type=F32)
    o_ref[...] = _rmsnorm_rows(y, g_ref[...]) if final_norm else y


def _ffn_down(a, w, x, g, tm, final_norm):
    m, kdim = a.shape
    d = w.shape[1]
    return pl.pallas_call(
        functools.partial(_ffn_down_kernel, final_norm=final_norm),
        grid=(m // tm,),
        in_specs=[pl.BlockSpec((tm, kdim), lambda i: (i, 0)),
                  pl.BlockSpec((kdim, d), lambda i: (0, 0), pipeline_mode=pl.Buffered(1)),
                  pl.BlockSpec((tm, d), lambda i: (i, 0)),
                  pl.BlockSpec((1, d), lambda i: (0, 0))],
        out_specs=pl.BlockSpec((tm, d), lambda i: (i, 0)),
        out_shape=jax.ShapeDtypeStruct((m, d), F32),
        compiler_params=_params(1),
        name="ffn_down_final_norm",
    )(a, w, x, g.reshape(1, d))


W_IN_SIZES = (MLA_Q_RANK, MLA_KV_RANK, MLA_ROPE, D_MODEL, D_MODEL, D_MODEL, 2 * D_MODEL)
W_IN_OFFS = tuple(sum(W_IN_SIZES[:n]) for n in range(len(W_IN_SIZES) + 1))


def _w_in_layout_kernel(wt_ref, o_ref, dvt_ref):
    o_cq, o_ckv, o_kpe, o_dq, o_dk, o_dv, o_gate, o_end = W_IN_OFFS
    half = MLA_ROPE // 2
    cols = o_ref.shape[1]

    def cast(lo, hi):
        return wt_ref[lo:hi, :].astype(BF16)

    o_ref[COL_CQ:COL_CQ + MLA_Q_RANK, :] = cast(o_cq, o_ckv)
    o_ref[COL_KPE:COL_KPE + MLA_ROPE, :] = cast(o_kpe, o_dq)
    o_ref[COL_KPE + MLA_ROPE:COL_KPE + MLA_ROPE + half, :] = (-wt_ref[o_kpe + half:o_dq, :]).astype(BF16)
    o_ref[COL_KPE + MLA_ROPE + half:COL_KPE + 2 * MLA_ROPE, :] = cast(o_kpe, o_kpe + half)
    o_ref[COL_KPE + 2 * MLA_ROPE:COL_CKV, :] = jnp.zeros((COL_CKV - COL_KPE - 2 * MLA_ROPE, cols), BF16)
    o_ref[COL_CKV:COL_CKV + MLA_KV_RANK, :] = cast(o_ckv, o_kpe)
    o_ref[COL_CKV + MLA_KV_RANK:COL_GATE, :] = jnp.zeros((COL_GATE - COL_CKV - MLA_KV_RANK, cols), BF16)
    o_ref[COL_GATE:COL_DQ, :] = cast(o_gate, o_end)
    o_ref[COL_DQ:COL_DK, :] = (wt_ref[o_dq:o_dk, :] * (LOG2E / math.sqrt(DIFF_HEAD_DIM))).astype(BF16)
    o_ref[COL_DK:PROJ_COLS, :] = cast(o_dk, o_dv)
    dvt_ref[...] = cast(o_dv, o_gate)


def _w_in_layout(w_in_t, tk):
    n, k = w_in_t.shape
    return pl.pallas_call(
        _w_in_layout_kernel,
        grid=(k // tk,),
        in_specs=[pl.BlockSpec((n, tk), lambda i: (0, i))],
        out_specs=[pl.BlockSpec((PROJ_COLS, tk), lambda i: (0, i)),
                   pl.BlockSpec((D_MODEL, tk), lambda i: (0, i))],
        out_shape=[jax.ShapeDtypeStruct((PROJ_COLS, k), BF16),
                   jax.ShapeDtypeStruct((D_MODEL, k), BF16)],
        compiler_params=_params(1),
        name="w_in_layout",
    )(w_in_t)


def _rotate_half_cols(w):
    half = w.shape[-1] // 2
    return jnp.concatenate([-w[..., half:], w[..., :half]], axis=-1)


def _prep_w_uq(w_uq):
    w = w_uq.reshape(MLA_Q_RANK, MLA_HEADS, MLA_QK)
    nope, rope = w[..., :MLA_NOPE], w[..., MLA_NOPE:]
    w = jnp.concatenate([nope, rope, _rotate_half_cols(rope)], axis=-1)
    return w.reshape(MLA_Q_RANK, MLA_HEADS * MLA_QPAD).astype(BF16)


def _prep_w_ukv(w_ukv):
    w = w_ukv.reshape(MLA_KV_RANK, MLA_HEADS, MLA_NOPE + MLA_V)
    k_nope = w[..., :MLA_NOPE].reshape(MLA_KV_RANK, MLA_HEADS * MLA_NOPE)
    v = w[..., MLA_NOPE:].reshape(MLA_KV_RANK, MLA_HEADS * MLA_V)
    return k_nope.astype(BF16), v.T.astype(BF16)


def _rope_tables(seq):
    inv = ROPE_THETA ** (-jnp.arange(0, MLA_ROPE, 2, dtype=F32) / MLA_ROPE)
    ang = jnp.arange(seq, dtype=F32)[:, None] * inv[None, :]
    cos, sin = jnp.cos(ang), jnp.sin(ang)
    zeros = jnp.zeros((seq, LANE - MLA_ROPE), F32)
    return (jnp.concatenate([cos, cos, zeros], axis=1), jnp.concatenate([sin, sin, zeros], axis=1))


def _tile(n, pref):
    t = pref
    while n % t:
        t //= 2
    return t


def kernel(x, attn_norm_g, w_in, b_gate, q_norm_g, w_uq, kv_norm_g, w_ukv, lambda_q1, lambda_k1,
           lambda_q2, lambda_k2, diff_norm_g, w_mla_proj, w_diff_proj, w_out, ffn_norm_g,
           w_ffn_gate, w_ffn_up, w_ffn_down, final_norm_g):
    batch, seq, d = x.shape
    depth = w_in.shape[0]
    t_attn = 512
    assert d == D_MODEL and seq % (4 * t_attn) == 0
    tokens = batch * seq
    xt = x.reshape(tokens, d)
    c1, c2 = _rope_tables(seq)
    slopes = jnp.exp2(-8.0 * jnp.arange(1, DIFF_HEADS + 1, dtype=F32) / DIFF_HEADS) * LOG2E
    tm = _tile(seq, 1024)

    for layer in range(depth):
        lambda_init = 0.8 - 0.6 * math.exp(-0.3 * layer)
        w1_t, w_dv_t = _w_in_layout(w_in[layer].T, 256)
        w_kn, w_v_t = _prep_w_ukv(w_ukv[layer])
        proj = _norm_matmul_wt(xt, attn_norm_g[layer], w1_t, tm, 2048)
        dvt = _norm_matmul_nt(xt, d, 0, attn_norm_g[layer], w_dv_t, tm, 2048)
        qm = _q_proj(proj, q_norm_g[layer], _prep_w_uq(w_uq[layer]), c1, c2, seq, tm, 2048)
        kn = _norm_matmul(proj, MLA_KV_RANK, COL_CKV // MLA_KV_RANK, kv_norm_g[layer], w_kn,
                          tm, 2048)
        vt = _norm_matmul_nt(proj, MLA_KV_RANK, COL_CKV // MLA_KV_RANK, kv_norm_g[layer], w_v_t,
                             tm, 2048)
        kr = _k_rope(proj, c1, c2, seq, tm)
        y_mla = _mla_attn(qm, kn, kr, vt, batch, seq, t_attn, 4 * t_attn)
        y_diff = _diff_attn(proj, dvt, slopes, lambda_q1[layer], lambda_k1[layer], lambda_q2[layer],
                            lambda_k2[layer], diff_norm_g[layer], batch, seq, t_attn, lambda_init)
        merged = _merge(y_mla, y_diff, w_mla_proj[layer].astype(BF16),
                        w_diff_proj[layer].astype(BF16), proj, b_gate[layer], tm, 1024)
        xt = _residual_matmul(merged, w_out[layer].astype(BF16), xt, tm, 1024)
        a = _ffn_up(xt, ffn_norm_g[layer], w_ffn_gate[layer], w_ffn_up[layer], tm, 512)
        xt = _ffn_down(a, w_ffn_down[layer].astype(BF16), xt, final_norm_g, 512,
                       final_norm=layer == depth - 1)
    return xt.reshape(batch, seq, d)
```

```python
import functools
import math

import jax
import jax.numpy as jnp
from jax import lax
from jax.experimental import pallas as pl
from jax.experimental.pallas import tpu as pltpu

D_MODEL = 2048
CHUNK = 64
EPS = 1e-6
NEG_INF = -1e30
LOG2E = math.log2(math.e)

MLA_HEADS = 16
MLA_Q_RANK = 768
MLA_KV_RANK = 512
MLA_NOPE = 128
MLA_ROPE = 64
MLA_V = 128
MLA_QK = MLA_NOPE + MLA_ROPE
ROPE_THETA = 10000.0

DIFF_HEADS = 8
DIFF_HEAD_DIM = 128
DIFF_V = 2 * DIFF_HEAD_DIM

LANE = 128
MLA_QPAD = 2 * LANE

COL_CQ = 0
COL_KPE = 768
COL_CKV = 1024
COL_GATE = 2048
COL_DQ = COL_GATE + 2 * D_MODEL
COL_DK = COL_DQ + D_MODEL
PROJ_COLS = COL_DK + D_MODEL

VMEM_LIMIT_BYTES = 56 * 1024 * 1024

BF16 = jnp.bfloat16
F32 = jnp.float32
NT_DIMS = (((1,), (1,)), ((), ()))


def _params(n_axes):
    return pltpu.CompilerParams(dimension_semantics=("arbitrary",) * n_axes,
                                vmem_limit_bytes=VMEM_LIMIT_BYTES)


def _rmsnorm_rows(x, g):
    ms = jnp.mean(x * x, axis=-1, keepdims=True)
    return x * lax.rsqrt(ms + EPS) * g


def _norm_matmul_kernel(x_ref, g_ref, w_ref, o_ref, h_ref):
    @pl.when(pl.program_id(1) == 0)
    def _():
        h_ref[...] = _rmsnorm_rows(x_ref[...].astype(F32), g_ref[...]).astype(BF16)

    o_ref[...] = jnp.dot(h_ref[...], w_ref[...], preferred_element_type=F32).astype(o_ref.dtype)


def _norm_matmul(x, x_cols, x_col_block, g, w, tm, tn):
    m = x.shape[0]
    n = w.shape[1]
    return pl.pallas_call(
        _norm_matmul_kernel,
        grid=(m // tm, n // tn),
        in_specs=[pl.BlockSpec((tm, x_cols), lambda i, j: (i, x_col_block)),
                  pl.BlockSpec((1, x_cols), lambda i, j: (0, 0)),
                  pl.BlockSpec((x_cols, tn), lambda i, j: (0, j))],
        out_specs=pl.BlockSpec((tm, tn), lambda i, j: (i, j)),
        out_shape=jax.ShapeDtypeStruct((m, n), BF16),
        scratch_shapes=[pltpu.VMEM((tm, x_cols), BF16)],
        compiler_params=_params(2),
        name="norm_matmul",
    )(x, g.reshape(1, x_cols), w)


def _norm_matmul_wt_kernel(x_ref, g_ref, wt_ref, o_ref, h_ref):
    @pl.when(pl.program_id(1) == 0)
    def _():
        h_ref[...] = _rmsnorm_rows(x_ref[...].astype(F32), g_ref[...]).astype(BF16)

    o_ref[...] = lax.dot_general(h_ref[...], wt_ref[...], NT_DIMS,
                                 preferred_element_type=F32).astype(o_ref.dtype)


def _norm_matmul_wt(x, g, wt, tm, tn):
    m, k = x.shape
    n = wt.shape[0]
    return pl.pallas_call(
        _norm_matmul_wt_kernel,
        grid=(m // tm, n // tn),
        in_specs=[pl.BlockSpec((tm, k), lambda i, j: (i, 0)),
                  pl.BlockSpec((1, k), lambda i, j: (0, 0)),
                  pl.BlockSpec((tn, k), lambda i, j: (j, 0))],
        out_specs=pl.BlockSpec((tm, tn), lambda i, j: (i, j)),
        out_shape=jax.ShapeDtypeStruct((m, n), BF16),
        scratch_shapes=[pltpu.VMEM((tm, k), BF16)],
        compiler_params=_params(2),
        name="in_proj",
    )(x, g.reshape(1, k), wt)


def _norm_matmul_nt_kernel(x_ref, g_ref, wt_ref, o_ref, h_ref):
    @pl.when(pl.program_id(1) == 0)
    def _():
        h_ref[...] = _rmsnorm_rows(x_ref[...].astype(F32), g_ref[...]).astype(BF16)

    o_ref[...] = lax.dot_general(wt_ref[...], h_ref[...], NT_DIMS,
                                 preferred_element_type=F32).astype(o_ref.dtype)


def _norm_matmul_nt(x, x_cols, x_col_block, g, wt, tm, tn):
    m = x.shape[0]
    n = wt.shape[0]
    return pl.pallas_call(
        _norm_matmul_nt_kernel,
        grid=(m // tm, n // tn),
        in_specs=[pl.BlockSpec((tm, x_cols), lambda i, j: (i, x_col_block)),
                  pl.BlockSpec((1, x_cols), lambda i, j: (0, 0)),
                  pl.BlockSpec((tn, x_cols), lambda i, j: (j, 0))],
        out_specs=pl.BlockSpec((tn, tm), lambda i, j: (j, i)),
        out_shape=jax.ShapeDtypeStruct((n, m), BF16),
        scratch_shapes=[pltpu.VMEM((tm, x_cols), BF16)],
        compiler_params=_params(2),
        name="norm_matmul_nt",
    )(x, g.reshape(1, x_cols), wt)


def _rope_block(blk, c1, c2):
    return blk * c1 + pltpu.roll(blk, LANE // 2, 1) * c2


def _q_proj_kernel(x_ref, g_ref, w_ref, c1_ref, c2_ref, o_ref, h_ref, *, heads_per_tile, qscale):
    @pl.when(pl.program_id(1) == 0)
    def _():
        h_ref[...] = _rmsnorm_rows(x_ref[...].astype(F32), g_ref[...]).astype(BF16)

    acc = jnp.dot(h_ref[...], w_ref[...], preferred_element_type=F32)
    c1 = c1_ref[...]
    c2 = c2_ref[...]
    for hh in range(heads_per_tile):
        lo = hh * MLA_QPAD
        o_ref[:, lo:lo + LANE] = (acc[:, lo:lo + LANE] * qscale).astype(o_ref.dtype)
        roped = _rope_block(acc[:, lo + LANE:lo + MLA_QPAD], c1, c2)
        o_ref[:, lo + LANE:lo + MLA_QPAD] = (roped * qscale).astype(o_ref.dtype)


def _q_proj(proj, g, w, c1, c2, seq, tm, tn):
    m = proj.shape[0]
    n = w.shape[1]
    s_tiles = seq // tm
    return pl.pallas_call(
        functools.partial(_q_proj_kernel, heads_per_tile=tn // MLA_QPAD,
                          qscale=LOG2E / math.sqrt(MLA_QK)),
        grid=(m // tm, n // tn),
        in_specs=[pl.BlockSpec((tm, MLA_Q_RANK), lambda i, j: (i, COL_CQ // MLA_Q_RANK)),
                  pl.BlockSpec((1, MLA_Q_RANK), lambda i, j: (0, 0)),
                  pl.BlockSpec((MLA_Q_RANK, tn), lambda i, j: (0, j)),
                  pl.BlockSpec((tm, LANE), lambda i, j: (i % s_tiles, 0)),
                  pl.BlockSpec((tm, LANE), lambda i, j: (i % s_tiles, 0))],
        out_specs=pl.BlockSpec((tm, tn), lambda i, j: (i, j)),
        out_shape=jax.ShapeDtypeStruct((m, n), BF16),
        scratch_shapes=[pltpu.VMEM((tm, MLA_Q_RANK), BF16)],
        compiler_params=_params(2),
        name="mla_q_proj",
    )(proj, g.reshape(1, MLA_Q_RANK), w, c1, c2)


def _k_rope_kernel(x_ref, c1_ref, c2_ref, o_ref):
    o_ref[...] = _rope_block(x_ref[...].astype(F32), c1_ref[...], c2_ref[...]).astype(o_ref.dtype)


def _k_rope(proj, c1, c2, seq, tm):
    m = proj.shape[0]
    s_tiles = seq // tm
    return pl.pallas_call(
        _k_rope_kernel,
        grid=(m // tm,),
        in_specs=[pl.BlockSpec((tm, LANE), lambda i: (i, COL_KPE // LANE)),
                  pl.BlockSpec((tm, LANE), lambda i: (i % s_tiles, 0)),
                  pl.BlockSpec((tm, LANE), lambda i: (i % s_tiles, 0))],
        out_specs=pl.BlockSpec((tm, LANE), lambda i: (i, 0)),
        out_shape=jax.ShapeDtypeStruct((m, LANE), BF16),
        compiler_params=_params(1),
        name="mla_k_rope",
    )(proj, c1, c2)


ONES_ROWS = 16
MLA_ONES_ROWS = 16


def _chunk_mask_t(t):
    shift = int(math.log2(CHUNK))
    kc = lax.shift_right_logical(lax.broadcasted_iota(jnp.int32, (t, t), 0), shift)
    qc = lax.shift_right_logical(lax.broadcasted_iota(jnp.int32, (t, t), 1), shift)
    return kc <= qc


def _softmax_update(s_t, vt, m_sc, acc_sc, cols=slice(None), col_max=None, shift=None,
                    ones_rows=ONES_ROWS):
    if col_max is None:
        col_max = jnp.max(s_t, axis=0, keepdims=True)
    if shift is not None:
        col_max = col_max + shift
    m_prev = m_sc[:, cols]
    m_new = jnp.maximum(m_prev, col_max)
    alpha = jnp.exp2(m_prev - m_new)
    m_sub = m_new if shift is None else m_new - shift
    p_t = jnp.exp2(s_t - m_sub).astype(BF16)
    v_aug = jnp.concatenate([vt, jnp.ones((ones_rows, vt.shape[1]), BF16)], axis=0)
    acc_sc[:, cols] = alpha * acc_sc[:, cols] + jnp.dot(v_aug, p_t, preferred_element_type=F32)
    m_sc[:, cols] = m_new


def _attend_query_tiles(n_tiles, n_diag, stages, bufs, pairs_per_trip=1):
    assert n_diag % 2 == 0
    stages.qk(0, 0, *bufs[0])

    def tile(i, carry):
        stages.init(i)

        def pair(p):
            j = 2 * p
            stages.qk(i, j + 1, *bufs[1])
            stages.pv(i, j, *bufs[0])
            stages.qk(i, j + 2, *bufs[0])
            stages.pv(i, j + 1, *bufs[1])

        def trip(p, carry):
            for u in range(pairs_per_trip):
                pair(pairs_per_trip * p + u)
            return carry

        def single(p, carry):
            pair(p)
            return carry

        n_pairs = (n_diag // 2) * i
        n_trips = lax.div(n_pairs, pairs_per_trip)
        lax.fori_loop(0, n_trips, trip, 0)
        if pairs_per_trip > 1:
            lax.fori_loop(n_trips * pairs_per_trip, n_pairs, single, 0)
        j = n_diag * i
        for d in range(n_diag):
            if d + 1 < n_diag:
                stages.qk_diag(i, j + d + 1, d + 1, *bufs[(d + 1) % 2])
            else:
                stages.qk(jnp.minimum(i + 1, n_tiles - 1), 0, *bufs[0])
            stages.diag(i, j + d, d, *bufs[d % 2])
        stages.finalize(i)
        return carry

    lax.fori_loop(0, n_tiles, tile, 0)


class _Stages:
    def __init__(self, **fns):
        self.__dict__.update(fns)


def _mla_attn_kernel(q_ref, kn_ref, kr_ref, vt_ref, o_ref, m_sc, acc_sc, fix_sc, s0_sc, s1_sc,
                     cm0_sc, cm1_sc, *, t, tq):
    fix_sc[...] = jnp.where(_chunk_mask_t(t), 0.0, NEG_INF)

    def key_rows(j):
        return pl.ds(pl.multiple_of(j * t, t), t)

    def qry_rows(i, lo=0):
        return pl.ds(pl.multiple_of(i * tq + lo, t), tq - lo)

    def keys(j):
        rows = key_rows(j)
        return jnp.concatenate([kn_ref[rows, :], kr_ref[rows, :]], axis=1)

    def init(i):
        m_sc[...] = jnp.full(m_sc.shape, NEG_INF, F32)
        acc_sc[...] = jnp.zeros(acc_sc.shape, F32)

    def qk(i, j, s_sc, cm_sc):
        s_t = lax.dot_general(keys(j), q_ref[qry_rows(i), :], NT_DIMS, preferred_element_type=F32)
        s_sc[...] = s_t
        cm_sc[...] = jnp.max(s_t, axis=0, keepdims=True)

    def qk_diag(i, j, d, s_sc, cm_sc):
        s_sc[:, d * t:] = lax.dot_general(keys(j), q_ref[qry_rows(i, d * t), :], NT_DIMS,
                                          preferred_element_type=F32)

    def pv(i, j, s_sc, cm_sc):
        _softmax_update(s_sc[...], vt_ref[:, key_rows(j)], m_sc, acc_sc, col_max=cm_sc[...],
                        ones_rows=MLA_ONES_ROWS)

    def diag(i, j, d, s_sc, cm_sc):
        lo = d * t
        parts = [s_sc[:, lo:lo + t] + fix_sc[...]]
        if lo + t < tq:
            parts.append(s_sc[:, lo + t:])
        _softmax_update(jnp.concatenate(parts, axis=1), vt_ref[:, key_rows(j)], m_sc, acc_sc,
                        cols=slice(lo, tq), ones_rows=MLA_ONES_ROWS)

    def finalize(i):
        acc = acc_sc[...]
        o_t = acc[:MLA_V] * (1.0 / acc[MLA_V:MLA_V + 1])
        o_ref[qry_rows(i), :] = o_t.T.astype(o_ref.dtype)

    stages = _Stages(init=init, qk=qk, qk_diag=qk_diag, pv=pv, diag=diag, finalize=finalize)
    _attend_query_tiles(q_ref.shape[0] // tq, tq // t, stages,
                        ((s0_sc, cm0_sc), (s1_sc, cm1_sc)), pairs_per_trip=2)


def _mla_attn(qm, kn, kr, vt, batch, seq, t, tq):
    return pl.pallas_call(
        functools.partial(_mla_attn_kernel, t=t, tq=tq),
        grid=(batch, MLA_HEADS),
        in_specs=[pl.BlockSpec((seq, MLA_QPAD), lambda b, h: (b, h)),
                  pl.BlockSpec((seq, LANE), lambda b, h: (b, h)),
                  pl.BlockSpec((seq, LANE), lambda b, h: (b, 0)),
                  pl.BlockSpec((MLA_V, seq), lambda b, h: (h, b))],
        out_specs=pl.BlockSpec((seq, MLA_V), lambda b, h: (b, h)),
        out_shape=jax.ShapeDtypeStruct((batch * seq, MLA_HEADS * MLA_V), BF16),
        scratch_shapes=[pltpu.VMEM((1, tq), F32),
                        pltpu.VMEM((MLA_V + MLA_ONES_ROWS, tq), F32),
                        pltpu.VMEM((t, t), F32),
                        pltpu.VMEM((t, tq), F32), pltpu.VMEM((t, tq), F32),
                        pltpu.VMEM((1, tq), F32), pltpu.VMEM((1, tq), F32)],
        compiler_params=_params(2),
        name="mla_attn",
    )(qm, kn, kr, vt)


def _diff_attn_kernel(slope_ref, q_ref, k_ref, vt_ref, lq1_ref, lk1_ref, lq2_ref, lk2_ref, sg_ref,
                      o_ref, m_sc, acc_sc, bias_sc, fix_sc, s0_sc, s1_sc, cm0_sc, cm1_sc, *, t,
                      lambda_init):
    d = DIFF_HEAD_DIM
    tq = 2 * t
    slope = slope_ref[pl.program_id(1)]

    key_idx = lax.broadcasted_iota(jnp.int32, (t, t), 0)
    qry_idx = lax.broadcasted_iota(jnp.int32, (t, t), 1)
    bias_sc[...] = key_idx.astype(F32) * slope
    fix = jnp.maximum(key_idx - qry_idx, 0).astype(F32) * (-2.0 * slope)
    fix_sc[...] = jnp.where(_chunk_mask_t(t), fix, NEG_INF)
    lam = (jnp.exp(jnp.sum(lq1_ref[...] * lk1_ref[...], axis=1, keepdims=True))
           - jnp.exp(jnp.sum(lq2_ref[...] * lk2_ref[...], axis=1, keepdims=True))
           + lambda_init)

    def key_rows(j):
        return pl.ds(pl.multiple_of(j * t, t), t)

    def qry_rows(i, lo=0):
        return pl.ds(pl.multiple_of(i * tq + lo, t), tq - lo)

    def scores(i, j, m, lo=0):
        lanes = slice(m * d, (m + 1) * d)
        s = lax.dot_general(k_ref[key_rows(j), lanes], q_ref[qry_rows(i, lo), lanes], NT_DIMS,
                            preferred_element_type=F32)
        return s + jnp.concatenate([bias_sc[...]] * (s.shape[1] // t), axis=1)

    def block_shift(i, j):
        return lax.convert_element_type(j * t - i * tq, F32) * slope

    def init(i):
        m_sc[...] = jnp.full(m_sc.shape, NEG_INF, F32)
        acc_sc[...] = jnp.zeros(acc_sc.shape, F32)

    def qk(i, j, s_sc, cm_sc):
        s_t = jnp.concatenate([scores(i, j, 0), scores(i, j, 1)], axis=1)
        s_sc[...] = s_t
        cm_sc[...] = jnp.max(s_t, axis=0, keepdims=True)

    def qk_diag(i, j, d, s_sc, cm_sc):
        for m in range(2):
            s_sc[:, m * tq + d * t:(m + 1) * tq] = scores(i, j, m, d * t)

    def pv(i, j, s_sc, cm_sc):
        _softmax_update(s_sc[...], vt_ref[:, key_rows(j)], m_sc, acc_sc, col_max=cm_sc[...],
                        shift=block_shift(i, j))

    def diag(i, j, d, s_sc, cm_sc):
        if d == 0:
            fix = fix_sc[...]
            s_t = jnp.concatenate([s_sc[:, :t] + fix, s_sc[:, t:tq], s_sc[:, tq:tq + t] + fix,
                                   s_sc[:, tq + t:]], axis=1)
            _softmax_update(s_t, vt_ref[:, key_rows(j)], m_sc, acc_sc, shift=block_shift(i, j))
            return
        for m in range(2):
            lo = m * tq + d * t
            parts = [s_sc[:, lo:lo + t] + fix_sc[...]]
            if lo + t < (m + 1) * tq:
                parts.append(s_sc[:, lo + t:(m + 1) * tq])
            _softmax_update(jnp.concatenate(parts, axis=1), vt_ref[:, key_rows(j)], m_sc, acc_sc,
                            cols=slice(lo, (m + 1) * tq), shift=block_shift(i, j))

    def finalize(i):
        acc = acc_sc[...]
        o_t = acc[:DIFF_V] * (1.0 / acc[DIFF_V:DIFF_V + 1])
        o = (o_t[:, :tq] - lam * o_t[:, tq:]).T
        y = _rmsnorm_rows(o, sg_ref[...]) * (1.0 - lambda_init)
        o_ref[qry_rows(i), :] = y.astype(o_ref.dtype)

    stages = _Stages(init=init, qk=qk, qk_diag=qk_diag, pv=pv, diag=diag, finalize=finalize)
    _attend_query_tiles(q_ref.shape[0] // tq, tq // t, stages,
                        ((s0_sc, cm0_sc), (s1_sc, cm1_sc)), pairs_per_trip=2)


def _diff_attn(proj, dvt, slopes, lq1, lk1, lq2, lk2, sub_g, batch, seq, t, lambda_init):
    tq = 2 * t
    vec = lambda a: a.reshape(1, -1).astype(F32)
    small = lambda n: pl.BlockSpec((1, n), lambda b, h, s: (0, 0))
    grid_spec = pltpu.PrefetchScalarGridSpec(
        num_scalar_prefetch=1,
        grid=(batch, DIFF_HEADS),
        in_specs=[pl.BlockSpec((seq, DIFF_V), lambda b, h, s: (b, COL_DQ // DIFF_V + h)),
                  pl.BlockSpec((seq, DIFF_V), lambda b, h, s: (b, COL_DK // DIFF_V + h)),
                  pl.BlockSpec((DIFF_V, seq), lambda b, h, s: (h, b)),
                  small(DIFF_HEAD_DIM), small(DIFF_HEAD_DIM), small(DIFF_HEAD_DIM),
                  small(DIFF_HEAD_DIM), small(DIFF_V)],
        out_specs=pl.BlockSpec((seq, DIFF_V), lambda b, h, s: (b, h)),
        scratch_shapes=[pltpu.VMEM((1, 2 * tq), F32),
                        pltpu.VMEM((DIFF_V + ONES_ROWS, 2 * tq), F32),
                        pltpu.VMEM((t, t), F32), pltpu.VMEM((t, t), F32),
                        pltpu.VMEM((t, 2 * tq), F32), pltpu.VMEM((t, 2 * tq), F32),
                        pltpu.VMEM((1, 2 * tq), F32), pltpu.VMEM((1, 2 * tq), F32)],
    )
    return pl.pallas_call(
        functools.partial(_diff_attn_kernel, t=t, lambda_init=lambda_init),
        grid_spec=grid_spec,
        out_shape=jax.ShapeDtypeStruct((batch * seq, DIFF_HEADS * DIFF_V), BF16),
        compiler_params=_params(2),
        name="diff_attn",
    )(slopes, proj, proj, dvt, vec(lq1), vec(lk1), vec(lq2), vec(lk2), vec(sub_g))


def _merge_kernel(ya_ref, yb_ref, wa_ref, wb_ref, ga_ref, gb_ref, ba_ref, bb_ref, o_ref):
    a = jnp.dot(ya_ref[...], wa_ref[...], preferred_element_type=F32)
    b = jnp.dot(yb_ref[...], wb_ref[...], preferred_element_type=F32)
    g0 = jax.nn.sigmoid(ga_ref[...].astype(F32) + ba_ref[...])
    g1 = jax.nn.sigmoid(gb_ref[...].astype(F32) + bb_ref[...])
    o_ref[...] = (g0 * a + g1 * b).astype(o_ref.dtype)


def _merge(y_mla, y_diff, w_mla, w_diff, proj, b_gate, tm, tn):
    m = y_mla.shape[0]
    d = D_MODEL
    g0_blk = COL_GATE // tn
    g1_blk = (COL_GATE + d) // tn
    nb = d // tn
    bg = b_gate.reshape(1, 2 * d).astype(F32)
    return pl.pallas_call(
        _merge_kernel,
        grid=(m // tm, nb),
        in_specs=[pl.BlockSpec((tm, d), lambda i, j: (i, 0)),
                  pl.BlockSpec((tm, d), lambda i, j: (i, 0)),
                  pl.BlockSpec((d, tn), lambda i, j: (0, j)),
                  pl.BlockSpec((d, tn), lambda i, j: (0, j)),
                  pl.BlockSpec((tm, tn), lambda i, j: (i, g0_blk + j)),
                  pl.BlockSpec((tm, tn), lambda i, j: (i, g1_blk + j)),
                  pl.BlockSpec((1, tn), lambda i, j: (0, j)),
                  pl.BlockSpec((1, tn), lambda i, j: (0, nb + j))],
        out_specs=pl.BlockSpec((tm, tn), lambda i, j: (i, j)),
        out_shape=jax.ShapeDtypeStruct((m, d), BF16),
        compiler_params=_params(2),
        name="gated_merge",
    )(y_mla, y_diff, w_mla, w_diff, proj, proj, bg, bg)


def _out_proj_kernel(a_ref, w_ref, x_ref, g_ref, x1_ref, h_ref):
    y = x_ref[...] + jnp.dot(a_ref[...], w_ref[...], preferred_element_type=F32)
    x1_ref[...] = y
    h_ref[...] = _rmsnorm_rows(y, g_ref[...]).astype(h_ref.dtype)


def _out_proj(a, w, x, g, tm):
    m, k = a.shape
    n = w.shape[1]
    return pl.pallas_call(
        _out_proj_kernel,
        grid=(m // tm,),
        in_specs=[pl.BlockSpec((tm, k), lambda i: (i, 0)),
                  pl.BlockSpec((k, n), lambda i: (0, 0), pipeline_mode=pl.Buffered(1)),
                  pl.BlockSpec((tm, n), lambda i: (i, 0)),
                  pl.BlockSpec((1, n), lambda i: (0, 0))],
        out_specs=[pl.BlockSpec((tm, n), lambda i: (i, 0)),
                   pl.BlockSpec((tm, n), lambda i: (i, 0))],
        out_shape=[jax.ShapeDtypeStruct((m, n), F32), jax.ShapeDtypeStruct((m, n), BF16)],
        compiler_params=_params(1),
        name="out_proj_residual",
    )(a, w, x, g.reshape(1, n))


def _ffn_up_kernel(h_ref, wg_ref, wu_ref, o_ref, wg_sc, wu_sc):
    @pl.when(pl.program_id(1) == 0)
    def _():
        wg_sc[...] = wg_ref[...].astype(BF16)
        wu_sc[...] = wu_ref[...].astype(BF16)

    h = h_ref[...]
    gate = jnp.dot(h, wg_sc[...], preferred_element_type=F32)
    up = jnp.dot(h, wu_sc[...], preferred_element_type=F32)
    o_ref[...] = (gate * jax.nn.sigmoid(gate) * up).astype(o_ref.dtype)


def _ffn_up(h, wg, wu, tm, tn):
    m, d = h.shape
    n = wg.shape[1]
    return pl.pallas_call(
        _ffn_up_kernel,
        grid=(n // tn, m // tm),
        in_specs=[pl.BlockSpec((tm, d), lambda j, i: (i, 0)),
                  pl.BlockSpec((d, tn), lambda j, i: (0, j)),
                  pl.BlockSpec((d, tn), lambda j, i: (0, j))],
        out_specs=pl.BlockSpec((tm, tn), lambda j, i: (i, j)),
        out_shape=jax.ShapeDtypeStruct((m, n), BF16),
        scratch_shapes=[pltpu.VMEM((d, tn), BF16), pltpu.VMEM((d, tn), BF16)],
        compiler_params=_params(2),
        name="ffn_up",
    )(h, wg, wu)


def _ffn_down_kernel(a_ref, w_ref, x_ref, g_ref, o_ref, *, final_norm):
    y = x_ref[...] + jnp.dot(a_ref[...], w_ref[...], preferred_element_type=F32)
    o_ref[...] = _rmsnorm_rows(y, g_ref[...]) if final_norm else y


def _ffn_down(a, w, x, g, tm, final_norm):
    m, kdim = a.shape
    d = w.shape[1]
    return pl.pallas_call(
        functools.partial(_ffn_down_kernel, final_norm=final_norm),
        grid=(m // tm,),
        in_specs=[pl.BlockSpec((tm, kdim), lambda i: (i, 0)),
                  pl.BlockSpec((kdim, d), lambda i: (0, 0), pipeline_mode=pl.Buffered(1)),
                  pl.BlockSpec((tm, d), lambda i: (i, 0)),
                  pl.BlockSpec((1, d), lambda i: (0, 0))],
        out_specs=pl.BlockSpec((tm, d), lambda i: (i, 0)),
        out_shape=jax.ShapeDtypeStruct((m, d), F32),
        compiler_params=_params(1),
        name="ffn_down_final_norm",
    )(a, w, x, g.reshape(1, d))


W_IN_SIZES = (MLA_Q_RANK, MLA_KV_RANK, MLA_ROPE, D_MODEL, D_MODEL, D_MODEL, 2 * D_MODEL)
W_IN_OFFS = tuple(sum(W_IN_SIZES[:n]) for n in range(len(W_IN_SIZES) + 1))


def _w_in_layout_kernel(wt_ref, o_ref, dvt_ref):
    o_cq, o_ckv, o_kpe, o_dq, o_dk, o_dv, o_gate, o_end = W_IN_OFFS
    half = MLA_ROPE // 2
    cols = o_ref.shape[1]

    def cast(lo, hi):
        return wt_ref[lo:hi, :].astype(BF16)

    o_ref[COL_CQ:COL_CQ + MLA_Q_RANK, :] = cast(o_cq, o_ckv)
    o_ref[COL_KPE:COL_KPE + MLA_ROPE, :] = cast(o_kpe, o_dq)
    o_ref[COL_KPE + MLA_ROPE:COL_KPE + MLA_ROPE + half, :] = (-wt_ref[o_kpe + half:o_dq, :]).astype(BF16)
    o_ref[COL_KPE + MLA_ROPE + half:COL_KPE + 2 * MLA_ROPE, :] = cast(o_kpe, o_kpe + half)
    o_ref[COL_KPE + 2 * MLA_ROPE:COL_CKV, :] = jnp.zeros((COL_CKV - COL_KPE - 2 * MLA_ROPE, cols), BF16)
    o_ref[COL_CKV:COL_CKV + MLA_KV_RANK, :] = cast(o_ckv, o_kpe)
    o_ref[COL_CKV + MLA_KV_RANK:COL_GATE, :] = jnp.zeros((COL_GATE - COL_CKV - MLA_KV_RANK, cols), BF16)
    o_ref[COL_GATE:COL_DQ, :] = cast(o_gate, o_end)
    o_ref[COL_DQ:COL_DK, :] = (wt_ref[o_dq:o_dk, :] * (LOG2E / math.sqrt(DIFF_HEAD_DIM))).astype(BF16)
    o_ref[COL_DK:PROJ_COLS, :] = cast(o_dk, o_dv)
    dvt_ref[...] = cast(o_dv, o_gate)


def _w_in_layout(w_in_t, tk):
    n, k = w_in_t.shape
    return pl.pallas_call(
        _w_in_layout_kernel,
        grid=(k // tk,),
        in_specs=[pl.BlockSpec((n, tk), lambda i: (0, i))],
        out_specs=[pl.BlockSpec((PROJ_COLS, tk), lambda i: (0, i)),
                   pl.BlockSpec((D_MODEL, tk), lambda i: (0, i))],
        out_shape=[jax.ShapeDtypeStruct((PROJ_COLS, k), BF16),
                   jax.ShapeDtypeStruct((D_MODEL, k), BF16)],
        compiler_params=_params(1),
        name="w_in_layout",
    )(w_in_t)


def _rotate_half_cols(w):
    half = w.shape[-1] // 2
    return jnp.concatenate([-w[..., half:], w[..., :half]], axis=-1)


def _prep_w_uq(w_uq):
    w = w_uq.reshape(MLA_Q_RANK, MLA_HEADS, MLA_QK)
    nope, rope = w[..., :MLA_NOPE], w[..., MLA_NOPE:]
    w = jnp.concatenate([nope, rope, _rotate_half_cols(rope)], axis=-1)
    return w.reshape(MLA_Q_RANK, MLA_HEADS * MLA_QPAD).astype(BF16)


def _prep_w_ukv(w_ukv):
    w = w_ukv.reshape(MLA_KV_RANK, MLA_HEADS, MLA_NOPE + MLA_V)
    k_nope = w[..., :MLA_NOPE].reshape(MLA_KV_RANK, MLA_HEADS * MLA_NOPE)
    v = w[..., MLA_NOPE:].reshape(MLA_KV_RANK, MLA_HEADS * MLA_V)
    return k_nope.astype(BF16), v.T.astype(BF16)


def _rope_tables(seq):
    inv = ROPE_THETA ** (-jnp.arange(0, MLA_ROPE, 2, dtype=F32) / MLA_ROPE)
    ang = jnp.arange(seq, dtype=F32)[:, None] * inv[None, :]
    cos, sin = jnp.cos(ang), jnp.sin(ang)
    zeros = jnp.zeros((seq, LANE - MLA_ROPE), F32)
    return (jnp.concatenate([cos, cos, zeros], axis=1), jnp.concatenate([sin, sin, zeros], axis=1))


def _tile(n, pref):
    t = pref
    while n % t:
        t //= 2
    return t


def kernel(x, attn_norm_g, w_in, b_gate, q_norm_g, w_uq, kv_norm_g, w_ukv, lambda_q1, lambda_k1,
           lambda_q2, lambda_k2, diff_norm_g, w_mla_proj, w_diff_proj, w_out, ffn_norm_g,
           w_ffn_gate, w_ffn_up, w_ffn_down, final_norm_g):
    batch, seq, d = x.shape
    depth = w_in.shape[0]
    t_attn = 512
    assert d == D_MODEL and seq % (4 * t_attn) == 0
    tokens = batch * seq
    xt = x.reshape(tokens, d)
    c1, c2 = _rope_tables(seq)
    slopes = jnp.exp2(-8.0 * jnp.arange(1, DIFF_HEADS + 1, dtype=F32) / DIFF_HEADS) * LOG2E
    tm = _tile(seq, 1024)

    for layer in range(depth):
        lambda_init = 0.8 - 0.6 * math.exp(-0.3 * layer)
        w1_t, w_dv_t = _w_in_layout(w_in[layer].T, 256)
        w_kn, w_v_t = _prep_w_ukv(w_ukv[layer])
        proj = _norm_matmul_wt(xt, attn_norm_g[layer], w1_t, tm, 2048)
        dvt = _norm_matmul_nt(xt, d, 0, attn_norm_g[layer], w_dv_t, tm, 2048)
        qm = _q_proj(proj, q_norm_g[layer], _prep_w_uq(w_uq[layer]), c1, c2, seq, tm, 2048)
        kn = _norm_matmul(proj, MLA_KV_RANK, COL_CKV // MLA_KV_RANK, kv_norm_g[layer], w_kn,
                          tm, 2048)
        vt = _norm_matmul_nt(proj, MLA_KV_RANK, COL_CKV // MLA_KV_RANK, kv_norm_g[layer], w_v_t,
                             tm, 2048)
        kr = _k_rope(proj, c1, c2, seq, tm)
        y_mla = _mla_attn(qm, kn, kr, vt, batch, seq, t_attn, 4 * t_attn)
        y_diff = _diff_attn(proj, dvt, slopes, lambda_q1[layer], lambda_k1[layer], lambda_q2[layer],
                            lambda_k2[layer], diff_norm_g[layer], batch, seq, t_attn, lambda_init)
        merged = _merge(y_mla, y_diff, w_mla_proj[layer].astype(BF16),
                        w_diff_proj[layer].astype(BF16), proj, b_gate[layer], tm, 1024)
        xt, h_ffn = _out_proj(merged, w_out[layer].astype(BF16), xt, ffn_norm_g[layer], 512)
        a = _ffn_up(h_ffn, w_ffn_gate[layer], w_ffn_up[layer], tm, 512)
        xt = _ffn_down(a, w_ffn_down[layer].astype(BF16), xt, final_norm_g, 512,
                       final_norm=layer == depth - 1)
    return xt.reshape(batch, seq, d)
```

```python
import functools
import math

import jax
import jax.numpy as jnp
from jax import lax
from jax.experimental import pallas as pl
from jax.experimental.pallas import tpu as pltpu

D_MODEL = 2048
CHUNK = 64
EPS = 1e-6
NEG_INF = -1e30
LOG2E = math.log2(math.e)

MLA_HEADS = 16
MLA_Q_RANK = 768
MLA_KV_RANK = 512
MLA_NOPE = 128
MLA_ROPE = 64
MLA_V = 128
MLA_QK = MLA_NOPE + MLA_ROPE
ROPE_THETA = 10000.0

DIFF_HEADS = 8
DIFF_HEAD_DIM = 128
DIFF_V = 2 * DIFF_HEAD_DIM

LANE = 128
MLA_QPAD = 2 * LANE

COL_CQ = 0
COL_KPE = 768
COL_CKV = 1024
COL_GATE = 2048
COL_DQ = COL_GATE + 2 * D_MODEL
COL_DK = COL_DQ + D_MODEL
PROJ_COLS = COL_DK + D_MODEL

VMEM_LIMIT_BYTES = 56 * 1024 * 1024
ROW_TILE = 1024
ROW_TILE_RESIDENT_W = 512
PROJ_COL_TILE = 2048
MERGE_COL_TILE = 1024
FFN_COL_TILE = 512
LAYOUT_COL_TILE = 256
ATTN_KEY_BLOCK = 512
MLA_QUERY_TILE = 4 * ATTN_KEY_BLOCK
DIFF_QUERY_TILE = 2 * ATTN_KEY_BLOCK

BF16 = jnp.bfloat16
F32 = jnp.float32
NT_DIMS = (((1,), (1,)), ((), ()))


def _params(n_axes):
    return pltpu.CompilerParams(dimension_semantics=("arbitrary",) * n_axes,
                                vmem_limit_bytes=VMEM_LIMIT_BYTES)


def _rmsnorm_rows(x, g):
    ms = jnp.mean(x * x, axis=-1, keepdims=True)
    return x * lax.rsqrt(ms + EPS) * g


def _norm_matmul_kernel(x_ref, g_ref, w_ref, o_ref, h_ref):
    @pl.when(pl.program_id(1) == 0)
    def _():
        h_ref[...] = _rmsnorm_rows(x_ref[...].astype(F32), g_ref[...]).astype(BF16)

    o_ref[...] = jnp.dot(h_ref[...], w_ref[...], preferred_element_type=F32).astype(o_ref.dtype)


def _norm_matmul(x, x_cols, x_col_block, g, w, tm, tn):
    m = x.shape[0]
    n = w.shape[1]
    return pl.pallas_call(
        _norm_matmul_kernel,
        grid=(m // tm, n // tn),
        in_specs=[pl.BlockSpec((tm, x_cols), lambda i, j: (i, x_col_block)),
                  pl.BlockSpec((1, x_cols), lambda i, j: (0, 0)),
                  pl.BlockSpec((x_cols, tn), lambda i, j: (0, j))],
        out_specs=pl.BlockSpec((tm, tn), lambda i, j: (i, j)),
        out_shape=jax.ShapeDtypeStruct((m, n), BF16),
        scratch_shapes=[pltpu.VMEM((tm, x_cols), BF16)],
        compiler_params=_params(2),
        name="norm_matmul",
    )(x, g.reshape(1, x_cols), w)


def _norm_matmul_wt_kernel(x_ref, g_ref, wt_ref, o_ref, h_ref):
    @pl.when(pl.program_id(1) == 0)
    def _():
        h_ref[...] = _rmsnorm_rows(x_ref[...].astype(F32), g_ref[...]).astype(BF16)

    o_ref[...] = lax.dot_general(h_ref[...], wt_ref[...], NT_DIMS,
                                 preferred_element_type=F32).astype(o_ref.dtype)


def _norm_matmul_wt(x, g, wt, tm, tn):
    m, k = x.shape
    n = wt.shape[0]
    return pl.pallas_call(
        _norm_matmul_wt_kernel,
        grid=(m // tm, n // tn),
        in_specs=[pl.BlockSpec((tm, k), lambda i, j: (i, 0)),
                  pl.BlockSpec((1, k), lambda i, j: (0, 0)),
                  pl.BlockSpec((tn, k), lambda i, j: (j, 0))],
        out_specs=pl.BlockSpec((tm, tn), lambda i, j: (i, j)),
        out_shape=jax.ShapeDtypeStruct((m, n), BF16),
        scratch_shapes=[pltpu.VMEM((tm, k), BF16)],
        compiler_params=_params(2),
        name="in_proj",
    )(x, g.reshape(1, k), wt)


def _norm_matmul_nt_kernel(x_ref, g_ref, wt_ref, o_ref, h_ref):
    @pl.when(pl.program_id(1) == 0)
    def _():
        h_ref[...] = _rmsnorm_rows(x_ref[...].astype(F32), g_ref[...]).astype(BF16)

    o_ref[...] = lax.dot_general(wt_ref[...], h_ref[...], NT_DIMS,
                                 preferred_element_type=F32).astype(o_ref.dtype)


def _norm_matmul_nt(x, x_cols, x_col_block, g, wt, tm, tn):
    m = x.shape[0]
    n = wt.shape[0]
    return pl.pallas_call(
        _norm_matmul_nt_kernel,
        grid=(m // tm, n // tn),
        in_specs=[pl.BlockSpec((tm, x_cols), lambda i, j: (i, x_col_block)),
                  pl.BlockSpec((1, x_cols), lambda i, j: (0, 0)),
                  pl.BlockSpec((tn, x_cols), lambda i, j: (j, 0))],
        out_specs=pl.BlockSpec((tn, tm), lambda i, j: (j, i)),
        out_shape=jax.ShapeDtypeStruct((n, m), BF16),
        scratch_shapes=[pltpu.VMEM((tm, x_cols), BF16)],
        compiler_params=_params(2),
        name="norm_matmul_nt",
    )(x, g.reshape(1, x_cols), wt)


def _rope_block(blk, c1, c2):
    return blk * c1 + pltpu.roll(blk, LANE // 2, 1) * c2


def _q_proj_kernel(x_ref, g_ref, w_ref, c1_ref, c2_ref, o_ref, h_ref, *, heads_per_tile, qscale):
    @pl.when(pl.program_id(1) == 0)
    def _():
        h_ref[...] = _rmsnorm_rows(x_ref[...].astype(F32), g_ref[...]).astype(BF16)

    acc = jnp.dot(h_ref[...], w_ref[...], preferred_element_type=F32)
    c1 = c1_ref[...]
    c2 = c2_ref[...]
    for hh in range(heads_per_tile):
        lo = hh * MLA_QPAD
        o_ref[:, lo:lo + LANE] = (acc[:, lo:lo + LANE] * qscale).astype(o_ref.dtype)
        roped = _rope_block(acc[:, lo + LANE:lo + MLA_QPAD], c1, c2)
        o_ref[:, lo + LANE:lo + MLA_QPAD] = (roped * qscale).astype(o_ref.dtype)


def _q_proj(proj, g, w, c1, c2, seq, tm, tn):
    m = proj.shape[0]
    n = w.shape[1]
    s_tiles = seq // tm
    return pl.pallas_call(
        functools.partial(_q_proj_kernel, heads_per_tile=tn // MLA_QPAD,
                          qscale=LOG2E / math.sqrt(MLA_QK)),
        grid=(m // tm, n // tn),
        in_specs=[pl.BlockSpec((tm, MLA_Q_RANK), lambda i, j: (i, COL_CQ // MLA_Q_RANK)),
                  pl.BlockSpec((1, MLA_Q_RANK), lambda i, j: (0, 0)),
                  pl.BlockSpec((MLA_Q_RANK, tn), lambda i, j: (0, j)),
                  pl.BlockSpec((tm, LANE), lambda i, j: (i % s_tiles, 0)),
                  pl.BlockSpec((tm, LANE), lambda i, j: (i % s_tiles, 0))],
        out_specs=pl.BlockSpec((tm, tn), lambda i, j: (i, j)),
        out_shape=jax.ShapeDtypeStruct((m, n), BF16),
        scratch_shapes=[pltpu.VMEM((tm, MLA_Q_RANK), BF16)],
        compiler_params=_params(2),
        name="mla_q_proj",
    )(proj, g.reshape(1, MLA_Q_RANK), w, c1, c2)


def _k_rope_kernel(x_ref, c1_ref, c2_ref, o_ref):
    o_ref[...] = _rope_block(x_ref[...].astype(F32), c1_ref[...], c2_ref[...]).astype(o_ref.dtype)


def _k_rope(proj, c1, c2, seq, tm):
    m = proj.shape[0]
    s_tiles = seq // tm
    return pl.pallas_call(
        _k_rope_kernel,
        grid=(m // tm,),
        in_specs=[pl.BlockSpec((tm, LANE), lambda i: (i, COL_KPE // LANE)),
                  pl.BlockSpec((tm, LANE), lambda i: (i % s_tiles, 0)),
                  pl.BlockSpec((tm, LANE), lambda i: (i % s_tiles, 0))],
        out_specs=pl.BlockSpec((tm, LANE), lambda i: (i, 0)),
        out_shape=jax.ShapeDtypeStruct((m, LANE), BF16),
        compiler_params=_params(1),
        name="mla_k_rope",
    )(proj, c1, c2)


ONES_ROWS = 16


def _chunk_mask_t(t):
    shift = int(math.log2(CHUNK))
    kc = lax.shift_right_logical(lax.broadcasted_iota(jnp.int32, (t, t), 0), shift)
    qc = lax.shift_right_logical(lax.broadcasted_iota(jnp.int32, (t, t), 1), shift)
    return kc <= qc


def _softmax_update(s_t, vt, m_sc, acc_sc, cols=slice(None), col_max=None, shift=None):
    if col_max is None:
        col_max = jnp.max(s_t, axis=0, keepdims=True)
    if shift is not None:
        col_max = col_max + shift
    m_prev = m_sc[:, cols]
    m_new = jnp.maximum(m_prev, col_max)
    alpha = jnp.exp2(m_prev - m_new)
    m_sub = m_new if shift is None else m_new - shift
    p_t = jnp.exp2(s_t - m_sub).astype(BF16)
    v_aug = jnp.concatenate([vt, jnp.ones((ONES_ROWS, vt.shape[1]), BF16)], axis=0)
    acc_sc[:, cols] = alpha * acc_sc[:, cols] + jnp.dot(v_aug, p_t, preferred_element_type=F32)
    m_sc[:, cols] = m_new


def _attend_query_tiles(n_tiles, n_diag, stages, bufs, pairs_per_trip=1):
    assert n_diag % 2 == 0
    stages.qk(0, 0, *bufs[0])

    def tile(i, carry):
        stages.init(i)

        def pair(p):
            j = 2 * p
            stages.qk(i, j + 1, *bufs[1])
            stages.pv(i, j, *bufs[0])
            stages.qk(i, j + 2, *bufs[0])
            stages.pv(i, j + 1, *bufs[1])

        def trip(p, carry):
            for u in range(pairs_per_trip):
                pair(pairs_per_trip * p + u)
            return carry

        def single(p, carry):
            pair(p)
            return carry

        n_pairs = (n_diag // 2) * i
        n_trips = lax.div(n_pairs, pairs_per_trip)
        lax.fori_loop(0, n_trips, trip, 0)
        if pairs_per_trip > 1:
            lax.fori_loop(n_trips * pairs_per_trip, n_pairs, single, 0)
        j = n_diag * i
        for d in range(n_diag):
            if d + 1 < n_diag:
                stages.qk_diag(i, j + d + 1, d + 1, *bufs[(d + 1) % 2])
            else:
                stages.qk(jnp.minimum(i + 1, n_tiles - 1), 0, *bufs[0])
            stages.diag(i, j + d, d, *bufs[d % 2])
        stages.finalize(i)
        return carry

    lax.fori_loop(0, n_tiles, tile, 0)


class _Stages:
    def __init__(self, **fns):
        self.__dict__.update(fns)


def _mla_attn_kernel(q_ref, kn_ref, kr_ref, vt_ref, o_ref, m_sc, acc_sc, fix_sc, s0_sc, s1_sc,
                     cm0_sc, cm1_sc, *, t, tq):
    fix_sc[...] = jnp.where(_chunk_mask_t(t), 0.0, NEG_INF)

    def key_rows(j):
        return pl.ds(pl.multiple_of(j * t, t), t)

    def qry_rows(i, lo=0):
        return pl.ds(pl.multiple_of(i * tq + lo, t), tq - lo)

    def keys(j):
        rows = key_rows(j)
        return jnp.concatenate([kn_ref[rows, :], kr_ref[rows, :]], axis=1)

    def init(i):
        m_sc[...] = jnp.full(m_sc.shape, NEG_INF, F32)
        acc_sc[...] = jnp.zeros(acc_sc.shape, F32)

    def qk(i, j, s_sc, cm_sc):
        s_t = lax.dot_general(keys(j), q_ref[qry_rows(i), :], NT_DIMS, preferred_element_type=F32)
        s_sc[...] = s_t
        cm_sc[...] = jnp.max(s_t, axis=0, keepdims=True)

    def qk_diag(i, j, d, s_sc, cm_sc):
        s_sc[:, d * t:] = lax.dot_general(keys(j), q_ref[qry_rows(i, d * t), :], NT_DIMS,
                                          preferred_element_type=F32)

    def pv(i, j, s_sc, cm_sc):
        _softmax_update(s_sc[...], vt_ref[:, key_rows(j)], m_sc, acc_sc, col_max=cm_sc[...])

    def diag(i, j, d, s_sc, cm_sc):
        lo = d * t
        parts = [s_sc[:, lo:lo + t] + fix_sc[...]]
        if lo + t < tq:
            parts.append(s_sc[:, lo + t:])
        _softmax_update(jnp.concatenate(parts, axis=1), vt_ref[:, key_rows(j)], m_sc, acc_sc,
                        cols=slice(lo, tq))

    def finalize(i):
        acc = acc_sc[...]
        o_t = acc[:MLA_V] * (1.0 / acc[MLA_V:MLA_V + 1])
        o_ref[qry_rows(i), :] = o_t.T.astype(o_ref.dtype)

    stages = _Stages(init=init, qk=qk, qk_diag=qk_diag, pv=pv, diag=diag, finalize=finalize)
    _attend_query_tiles(q_ref.shape[0] // tq, tq // t, stages,
                        ((s0_sc, cm0_sc), (s1_sc, cm1_sc)), pairs_per_trip=2)


def _mla_attn(qm, kn, kr, vt, batch, seq, t, tq):
    return pl.pallas_call(
        functools.partial(_mla_attn_kernel, t=t, tq=tq),
        grid=(batch, MLA_HEADS),
        in_specs=[pl.BlockSpec((seq, MLA_QPAD), lambda b, h: (b, h)),
                  pl.BlockSpec((seq, LANE), lambda b, h: (b, h)),
                  pl.BlockSpec((seq, LANE), lambda b, h: (b, 0)),
                  pl.BlockSpec((MLA_V, seq), lambda b, h: (h, b))],
        out_specs=pl.BlockSpec((seq, MLA_V), lambda b, h: (b, h)),
        out_shape=jax.ShapeDtypeStruct((batch * seq, MLA_HEADS * MLA_V), BF16),
        scratch_shapes=[pltpu.VMEM((1, tq), F32),
                        pltpu.VMEM((MLA_V + ONES_ROWS, tq), F32),
                        pltpu.VMEM((t, t), F32),
                        pltpu.VMEM((t, tq), F32), pltpu.VMEM((t, tq), F32),
                        pltpu.VMEM((1, tq), F32), pltpu.VMEM((1, tq), F32)],
        compiler_params=_params(2),
        name="mla_attn",
    )(qm, kn, kr, vt)


def _diff_attn_kernel(slope_ref, q_ref, k_ref, vt_ref, lq1_ref, lk1_ref, lq2_ref, lk2_ref, sg_ref,
                      o_ref, m_sc, acc_sc, bias_sc, fix_sc, s0_sc, s1_sc, cm0_sc, cm1_sc, *, t, tq,
                      lambda_init):
    d = DIFF_HEAD_DIM
    slope = slope_ref[pl.program_id(1)]

    key_idx = lax.broadcasted_iota(jnp.int32, (t, t), 0)
    qry_idx = lax.broadcasted_iota(jnp.int32, (t, t), 1)
    bias_sc[...] = key_idx.astype(F32) * slope
    fix = jnp.maximum(key_idx - qry_idx, 0).astype(F32) * (-2.0 * slope)
    fix_sc[...] = jnp.where(_chunk_mask_t(t), fix, NEG_INF)
    lam = (jnp.exp(jnp.sum(lq1_ref[...] * lk1_ref[...], axis=1, keepdims=True))
           - jnp.exp(jnp.sum(lq2_ref[...] * lk2_ref[...], axis=1, keepdims=True))
           + lambda_init)

    def key_rows(j):
        return pl.ds(pl.multiple_of(j * t, t), t)

    def qry_rows(i, lo=0):
        return pl.ds(pl.multiple_of(i * tq + lo, t), tq - lo)

    def scores(i, j, m, lo=0):
        lanes = slice(m * d, (m + 1) * d)
        s = lax.dot_general(k_ref[key_rows(j), lanes], q_ref[qry_rows(i, lo), lanes], NT_DIMS,
                            preferred_element_type=F32)
        return s + jnp.concatenate([bias_sc[...]] * (s.shape[1] // t), axis=1)

    def block_shift(i, j):
        return lax.convert_element_type(j * t - i * tq, F32) * slope

    def init(i):
        m_sc[...] = jnp.full(m_sc.shape, NEG_INF, F32)
        acc_sc[...] = jnp.zeros(acc_sc.shape, F32)

    def qk(i, j, s_sc, cm_sc):
        s_t = jnp.concatenate([scores(i, j, 0), scores(i, j, 1)], axis=1)
        s_sc[...] = s_t
        cm_sc[...] = jnp.max(s_t, axis=0, keepdims=True)

    def qk_diag(i, j, d, s_sc, cm_sc):
        for m in range(2):
            s_sc[:, m * tq + d * t:(m + 1) * tq] = scores(i, j, m, d * t)

    def pv(i, j, s_sc, cm_sc):
        _softmax_update(s_sc[...], vt_ref[:, key_rows(j)], m_sc, acc_sc, col_max=cm_sc[...],
                        shift=block_shift(i, j))

    def diag(i, j, d, s_sc, cm_sc):
        if d == 0:
            fix = fix_sc[...]
            s_t = jnp.concatenate([s_sc[:, :t] + fix, s_sc[:, t:tq], s_sc[:, tq:tq + t] + fix,
                                   s_sc[:, tq + t:]], axis=1)
            _softmax_update(s_t, vt_ref[:, key_rows(j)], m_sc, acc_sc, shift=block_shift(i, j))
            return
        for m in range(2):
            lo = m * tq + d * t
            parts = [s_sc[:, lo:lo + t] + fix_sc[...]]
            if lo + t < (m + 1) * tq:
                parts.append(s_sc[:, lo + t:(m + 1) * tq])
            _softmax_update(jnp.concatenate(parts, axis=1), vt_ref[:, key_rows(j)], m_sc, acc_sc,
                            cols=slice(lo, (m + 1) * tq), shift=block_shift(i, j))

    def finalize(i):
        acc = acc_sc[...]
        o_t = acc[:DIFF_V] * (1.0 / acc[DIFF_V:DIFF_V + 1])
        o = (o_t[:, :tq] - lam * o_t[:, tq:]).T
        y = _rmsnorm_rows(o, sg_ref[...]) * (1.0 - lambda_init)
        o_ref[qry_rows(i), :] = y.astype(o_ref.dtype)

    stages = _Stages(init=init, qk=qk, qk_diag=qk_diag, pv=pv, diag=diag, finalize=finalize)
    _attend_query_tiles(q_ref.shape[0] // tq, tq // t, stages,
                        ((s0_sc, cm0_sc), (s1_sc, cm1_sc)), pairs_per_trip=2)


def _diff_attn(proj, dvt, slopes, lq1, lk1, lq2, lk2, sub_g, batch, seq, t, tq, lambda_init):
    vec = lambda a: a.reshape(1, -1).astype(F32)
    small = lambda n: pl.BlockSpec((1, n), lambda b, h, s: (0, 0))
    grid_spec = pltpu.PrefetchScalarGridSpec(
        num_scalar_prefetch=1,
        grid=(batch, DIFF_HEADS),
        in_specs=[pl.BlockSpec((seq, DIFF_V), lambda b, h, s: (b, COL_DQ // DIFF_V + h)),
                  pl.BlockSpec((seq, DIFF_V), lambda b, h, s: (b, COL_DK // DIFF_V + h)),
                  pl.BlockSpec((DIFF_V, seq), lambda b, h, s: (h, b)),
                  small(DIFF_HEAD_DIM), small(DIFF_HEAD_DIM), small(DIFF_HEAD_DIM),
                  small(DIFF_HEAD_DIM), small(DIFF_V)],
        out_specs=pl.BlockSpec((seq, DIFF_V), lambda b, h, s: (b, h)),
        scratch_shapes=[pltpu.VMEM((1, 2 * tq), F32),
                        pltpu.VMEM((DIFF_V + ONES_ROWS, 2 * tq), F32),
                        pltpu.VMEM((t, t), F32), pltpu.VMEM((t, t), F32),
                        pltpu.VMEM((t, 2 * tq), F32), pltpu.VMEM((t, 2 * tq), F32),
                        pltpu.VMEM((1, 2 * tq), F32), pltpu.VMEM((1, 2 * tq), F32)],
    )
    return pl.pallas_call(
        functools.partial(_diff_attn_kernel, t=t, tq=tq, lambda_init=lambda_init),
        grid_spec=grid_spec,
        out_shape=jax.ShapeDtypeStruct((batch * seq, DIFF_HEADS * DIFF_V), BF16),
        compiler_params=_params(2),
        name="diff_attn",
    )(slopes, proj, proj, dvt, vec(lq1), vec(lk1), vec(lq2), vec(lk2), vec(sub_g))


def _merge_kernel(ya_ref, yb_ref, wa_ref, wb_ref, ga_ref, gb_ref, ba_ref, bb_ref, o_ref):
    a = jnp.dot(ya_ref[...], wa_ref[...], preferred_element_type=F32)
    b = jnp.dot(yb_ref[...], wb_ref[...], preferred_element_type=F32)
    g0 = jax.nn.sigmoid(ga_ref[...].astype(F32) + ba_ref[...])
    g1 = jax.nn.sigmoid(gb_ref[...].astype(F32) + bb_ref[...])
    o_ref[...] = (g0 * a + g1 * b).astype(o_ref.dtype)


def _merge(y_mla, y_diff, w_mla, w_diff, proj, b_gate, tm, tn):
    m = y_mla.shape[0]
    d = D_MODEL
    g0_blk = COL_GATE // tn
    g1_blk = (COL_GATE + d) // tn
    nb = d // tn
    bg = b_gate.reshape(1, 2 * d).astype(F32)
    return pl.pallas_call(
        _merge_kernel,
        grid=(m // tm, nb),
        in_specs=[pl.BlockSpec((tm, d), lambda i, j: (i, 0)),
                  pl.BlockSpec((tm, d), lambda i, j: (i, 0)),
                  pl.BlockSpec((d, tn), lambda i, j: (0, j)),
                  pl.BlockSpec((d, tn), lambda i, j: (0, j)),
                  pl.BlockSpec((tm, tn), lambda i, j: (i, g0_blk + j)),
                  pl.BlockSpec((tm, tn), lambda i, j: (i, g1_blk + j)),
                  pl.BlockSpec((1, tn), lambda i, j: (0, j)),
                  pl.BlockSpec((1, tn), lambda i, j: (0, nb + j))],
        out_specs=pl.BlockSpec((tm, tn), lambda i, j: (i, j)),
        out_shape=jax.ShapeDtypeStruct((m, d), BF16),
        compiler_params=_params(2),
        name="gated_merge",
    )(y_mla, y_diff, w_mla, w_diff, proj, proj, bg, bg)


def _out_proj_kernel(a_ref, w_ref, x_ref, g_ref, x1_ref, h_ref):
    y = x_ref[...] + jnp.dot(a_ref[...], w_ref[...], preferred_element_type=F32)
    x1_ref[...] = y
    h_ref[...] = _rmsnorm_rows(y, g_ref[...]).astype(h_ref.dtype)


def _out_proj(a, w, x, g, tm):
    m, k = a.shape
    n = w.shape[1]
    return pl.pallas_call(
        _out_proj_kernel,
        grid=(m // tm,),
        in_specs=[pl.BlockSpec((tm, k), lambda i: (i, 0)),
                  pl.BlockSpec((k, n), lambda i: (0, 0), pipeline_mode=pl.Buffered(1)),
                  pl.BlockSpec((tm, n), lambda i: (i, 0)),
                  pl.BlockSpec((1, n), lambda i: (0, 0))],
        out_specs=[pl.BlockSpec((tm, n), lambda i: (i, 0)),
                   pl.BlockSpec((tm, n), lambda i: (i, 0))],
        out_shape=[jax.ShapeDtypeStruct((m, n), F32), jax.ShapeDtypeStruct((m, n), BF16)],
        compiler_params=_params(1),
        name="out_proj_residual",
    )(a, w, x, g.reshape(1, n))


def _ffn_up_kernel(h_ref, wg_ref, wu_ref, o_ref, wg_sc, wu_sc):
    @pl.when(pl.program_id(1) == 0)
    def _():
        wg_sc[...] = wg_ref[...].astype(BF16)
        wu_sc[...] = wu_ref[...].astype(BF16)

    h = h_ref[...]
    gate = jnp.dot(h, wg_sc[...], preferred_element_type=F32)
    up = jnp.dot(h, wu_sc[...], preferred_element_type=F32)
    o_ref[...] = (gate * jax.nn.sigmoid(gate) * up).astype(o_ref.dtype)


def _ffn_up(h, wg, wu, tm, tn):
    m, d = h.shape
    n = wg.shape[1]
    return pl.pallas_call(
        _ffn_up_kernel,
        grid=(n // tn, m // tm),
        in_specs=[pl.BlockSpec((tm, d), lambda j, i: (i, 0)),
                  pl.BlockSpec((d, tn), lambda j, i: (0, j)),
                  pl.BlockSpec((d, tn), lambda j, i: (0, j))],
        out_specs=pl.BlockSpec((tm, tn), lambda j, i: (i, j)),
        out_shape=jax.ShapeDtypeStruct((m, n), BF16),
        scratch_shapes=[pltpu.VMEM((d, tn), BF16), pltpu.VMEM((d, tn), BF16)],
        compiler_params=_params(2),
        name="ffn_up",
    )(h, wg, wu)


def _ffn_down_kernel(a_ref, w_ref, x_ref, g_ref, o_ref, *, final_norm):
    y = x_ref[...] + jnp.dot(a_ref[...], w_ref[...], preferred_element_type=F32)
    o_ref[...] = _rmsnorm_rows(y, g_ref[...]) if final_norm else y


def _ffn_down(a, w, x, g, tm, final_norm):
    m, kdim = a.shape
    d = w.shape[1]
    return pl.pallas_call(
        functools.partial(_ffn_down_kernel, final_norm=final_norm),
        grid=(m // tm,),
        in_specs=[pl.BlockSpec((tm, kdim), lambda i: (i, 0)),
                  pl.BlockSpec((kdim, d), lambda i: (0, 0), pipeline_mode=pl.Buffered(1)),
                  pl.BlockSpec((tm, d), lambda i: (i, 0)),
                  pl.BlockSpec((1, d), lambda i: (0, 0))],
        out_specs=pl.BlockSpec((tm, d), lambda i: (i, 0)),
        out_shape=jax.ShapeDtypeStruct((m, d), F32),
        compiler_params=_params(1),
        name="ffn_down_final_norm",
    )(a, w, x, g.reshape(1, d))


W_IN_SIZES = (MLA_Q_RANK, MLA_KV_RANK, MLA_ROPE, D_MODEL, D_MODEL, D_MODEL, 2 * D_MODEL)
W_IN_OFFS = tuple(sum(W_IN_SIZES[:n]) for n in range(len(W_IN_SIZES) + 1))


def _w_in_layout_kernel(wt_ref, o_ref, dvt_ref):
    o_cq, o_ckv, o_kpe, o_dq, o_dk, o_dv, o_gate, o_end = W_IN_OFFS
    half = MLA_ROPE // 2
    cols = o_ref.shape[1]

    def cast(lo, hi):
        return wt_ref[lo:hi, :].astype(BF16)

    o_ref[COL_CQ:COL_CQ + MLA_Q_RANK, :] = cast(o_cq, o_ckv)
    o_ref[COL_KPE:COL_KPE + MLA_ROPE, :] = cast(o_kpe, o_dq)
    o_ref[COL_KPE + MLA_ROPE:COL_KPE + MLA_ROPE + half, :] = (-wt_ref[o_kpe + half:o_dq, :]).astype(BF16)
    o_ref[COL_KPE + MLA_ROPE + half:COL_KPE + 2 * MLA_ROPE, :] = cast(o_kpe, o_kpe + half)
    o_ref[COL_KPE + 2 * MLA_ROPE:COL_CKV, :] = jnp.zeros((COL_CKV - COL_KPE - 2 * MLA_ROPE, cols), BF16)
    o_ref[COL_CKV:COL_CKV + MLA_KV_RANK, :] = cast(o_ckv, o_kpe)
    o_ref[COL_CKV + MLA_KV_RANK:COL_GATE, :] = jnp.zeros((COL_GATE - COL_CKV - MLA_KV_RANK, cols), BF16)
    o_ref[COL_GATE:COL_DQ, :] = cast(o_gate, o_end)
    o_ref[COL_DQ:COL_DK, :] = (wt_ref[o_dq:o_dk, :] * (LOG2E / math.sqrt(DIFF_HEAD_DIM))).astype(BF16)
    o_ref[COL_DK:PROJ_COLS, :] = cast(o_dk, o_dv)
    dvt_ref[...] = cast(o_dv, o_gate)


def _w_in_layout(w_in_t, tk):
    n, k = w_in_t.shape
    return pl.pallas_call(
        _w_in_layout_kernel,
        grid=(k // tk,),
        in_specs=[pl.BlockSpec((n, tk), lambda i: (0, i))],
        out_specs=[pl.BlockSpec((PROJ_COLS, tk), lambda i: (0, i)),
                   pl.BlockSpec((D_MODEL, tk), lambda i: (0, i))],
        out_shape=[jax.ShapeDtypeStruct((PROJ_COLS, k), BF16),
                   jax.ShapeDtypeStruct((D_MODEL, k), BF16)],
        compiler_params=_params(1),
        name="w_in_layout",
    )(w_in_t)


def _rotate_half_cols(w):
    half = w.shape[-1] // 2
    return jnp.concatenate([-w[..., half:], w[..., :half]], axis=-1)


def _prep_w_uq(w_uq):
    w = w_uq.reshape(MLA_Q_RANK, MLA_HEADS, MLA_QK)
    nope, rope = w[..., :MLA_NOPE], w[..., MLA_NOPE:]
    w = jnp.concatenate([nope, rope, _rotate_half_cols(rope)], axis=-1)
    return w.reshape(MLA_Q_RANK, MLA_HEADS * MLA_QPAD).astype(BF16)


def _prep_w_ukv(w_ukv):
    w = w_ukv.reshape(MLA_KV_RANK, MLA_HEADS, MLA_NOPE + MLA_V)
    k_nope = w[..., :MLA_NOPE].reshape(MLA_KV_RANK, MLA_HEADS * MLA_NOPE)
    v = w[..., MLA_NOPE:].reshape(MLA_KV_RANK, MLA_HEADS * MLA_V)
    return k_nope.astype(BF16), v.T.astype(BF16)


def _rope_tables(seq):
    inv = ROPE_THETA ** (-jnp.arange(0, MLA_ROPE, 2, dtype=F32) / MLA_ROPE)
    ang = jnp.arange(seq, dtype=F32)[:, None] * inv[None, :]
    cos, sin = jnp.cos(ang), jnp.sin(ang)
    zeros = jnp.zeros((seq, LANE - MLA_ROPE), F32)
    return (jnp.concatenate([cos, cos, zeros], axis=1), jnp.concatenate([sin, sin, zeros], axis=1))


def kernel(x, attn_norm_g, w_in, b_gate, q_norm_g, w_uq, kv_norm_g, w_ukv, lambda_q1, lambda_k1,
           lambda_q2, lambda_k2, diff_norm_g, w_mla_proj, w_diff_proj, w_out, ffn_norm_g,
           w_ffn_gate, w_ffn_up, w_ffn_down, final_norm_g):
    batch, seq, d = x.shape
    depth = w_in.shape[0]
    assert d == D_MODEL and seq % MLA_QUERY_TILE == 0 and seq % ROW_TILE == 0
    tokens = batch * seq
    xt = x.reshape(tokens, d)
    c1, c2 = _rope_tables(seq)
    slopes = jnp.exp2(-8.0 * jnp.arange(1, DIFF_HEADS + 1, dtype=F32) / DIFF_HEADS) * LOG2E
    tm, t = ROW_TILE, ATTN_KEY_BLOCK
    ckv_block = COL_CKV // MLA_KV_RANK

    for layer in range(depth):
        lambda_init = 0.8 - 0.6 * math.exp(-0.3 * layer)
        w1_t, w_dv_t = _w_in_layout(w_in[layer].T, LAYOUT_COL_TILE)
        w_kn, w_v_t = _prep_w_ukv(w_ukv[layer])
        proj = _norm_matmul_wt(xt, attn_norm_g[layer], w1_t, tm, PROJ_COL_TILE)
        dvt = _norm_matmul_nt(xt, d, 0, attn_norm_g[layer], w_dv_t, tm, PROJ_COL_TILE)
        qm = _q_proj(proj, q_norm_g[layer], _prep_w_uq(w_uq[layer]), c1, c2, seq, tm,
                     PROJ_COL_TILE)
        kn = _norm_matmul(proj, MLA_KV_RANK, ckv_block, kv_norm_g[layer], w_kn, tm, PROJ_COL_TILE)
        vt = _norm_matmul_nt(proj, MLA_KV_RANK, ckv_block, kv_norm_g[layer], w_v_t, tm,
                             PROJ_COL_TILE)
        kr = _k_rope(proj, c1, c2, seq, tm)
        y_mla = _mla_attn(qm, kn, kr, vt, batch, seq, t, MLA_QUERY_TILE)
        y_diff = _diff_attn(proj, dvt, slopes, lambda_q1[layer], lambda_k1[layer], lambda_q2[layer],
                            lambda_k2[layer], diff_norm_g[layer], batch, seq, t, DIFF_QUERY_TILE,
                            lambda_init)
        merged = _merge(y_mla, y_diff, w_mla_proj[layer].astype(BF16),
                        w_diff_proj[layer].astype(BF16), proj, b_gate[layer], tm, MERGE_COL_TILE)
        xt, h_ffn = _out_proj(merged, w_out[layer].astype(BF16), xt, ffn_norm_g[layer],
                              ROW_TILE_RESIDENT_W)
        a = _ffn_up(h_ffn, w_ffn_gate[layer], w_ffn_up[layer], tm, FFN_COL_TILE)
        xt = _ffn_down(a, w_ffn_down[layer].astype(BF16), xt, final_norm_g, ROW_TILE_RESIDENT_W,
                       final_norm=layer == depth - 1)
    return xt.reshape(batch, seq, d)
```

```python
import functools
import math

import jax
import jax.numpy as jnp
from jax import lax
from jax.experimental import pallas as pl
from jax.experimental.pallas import tpu as pltpu

D_MODEL = 2048
CHUNK = 64
EPS = 1e-6
NEG_INF = -1e30
LOG2E = math.log2(math.e)

MLA_HEADS = 16
MLA_Q_RANK = 768
MLA_KV_RANK = 512
MLA_NOPE = 128
MLA_ROPE = 64
MLA_V = 128
MLA_QK = MLA_NOPE + MLA_ROPE
ROPE_THETA = 10000.0

DIFF_HEADS = 8
DIFF_HEAD_DIM = 128
DIFF_V = 2 * DIFF_HEAD_DIM

LANE = 128
MLA_QPAD = 2 * LANE

COL_CQ = 0
COL_KPE = 768
COL_CKV = 1024
COL_GATE = 2048
COL_DQ = COL_GATE + 2 * D_MODEL
COL_DK = COL_DQ + D_MODEL
PROJ_COLS = COL_DK + D_MODEL

VMEM_LIMIT_BYTES = 56 * 1024 * 1024
ROW_TILE = 1024
ROW_TILE_RESIDENT_W = 512
PROJ_COL_TILE = 2048
MERGE_COL_TILE = 1024
FFN_COL_TILE = 512
LAYOUT_COL_TILE = 256
ATTN_KEY_BLOCK = 512
MLA_QUERY_TILE = 4 * ATTN_KEY_BLOCK
DIFF_QUERY_TILE = 2 * ATTN_KEY_BLOCK

BF16 = jnp.bfloat16
F32 = jnp.float32
NT_DIMS = (((1,), (1,)), ((), ()))


def _params(n_axes):
    return pltpu.CompilerParams(dimension_semantics=("arbitrary",) * n_axes,
                                vmem_limit_bytes=VMEM_LIMIT_BYTES)


def _rmsnorm_rows(x, g):
    ms = jnp.mean(x * x, axis=-1, keepdims=True)
    return x * lax.rsqrt(ms + EPS) * g


def _norm_matmul_kernel(x_ref, g_ref, w_ref, o_ref, h_ref):
    @pl.when(pl.program_id(1) == 0)
    def _():
        h_ref[...] = _rmsnorm_rows(x_ref[...].astype(F32), g_ref[...]).astype(BF16)

    o_ref[...] = jnp.dot(h_ref[...], w_ref[...], preferred_element_type=F32).astype(o_ref.dtype)


def _norm_matmul(x, x_cols, x_col_block, g, w, tm, tn):
    m = x.shape[0]
    n = w.shape[1]
    return pl.pallas_call(
        _norm_matmul_kernel,
        grid=(m // tm, n // tn),
        in_specs=[pl.BlockSpec((tm, x_cols), lambda i, j: (i, x_col_block)),
                  pl.BlockSpec((1, x_cols), lambda i, j: (0, 0)),
                  pl.BlockSpec((x_cols, tn), lambda i, j: (0, j))],
        out_specs=pl.BlockSpec((tm, tn), lambda i, j: (i, j)),
        out_shape=jax.ShapeDtypeStruct((m, n), BF16),
        scratch_shapes=[pltpu.VMEM((tm, x_cols), BF16)],
        compiler_params=_params(2),
        name="norm_matmul",
    )(x, g.reshape(1, x_cols), w)


def _norm_matmul_wt_kernel(x_ref, g_ref, wt_ref, o_ref, h_ref):
    @pl.when(pl.program_id(1) == 0)
    def _():
        h_ref[...] = _rmsnorm_rows(x_ref[...].astype(F32), g_ref[...]).astype(BF16)

    o_ref[...] = lax.dot_general(h_ref[...], wt_ref[...], NT_DIMS,
                                 preferred_element_type=F32).astype(o_ref.dtype)


def _norm_matmul_wt(x, g, wt, tm, tn):
    m, k = x.shape
    n = wt.shape[0]
    return pl.pallas_call(
        _norm_matmul_wt_kernel,
        grid=(m // tm, n // tn),
        in_specs=[pl.BlockSpec((tm, k), lambda i, j: (i, 0)),
                  pl.BlockSpec((1, k), lambda i, j: (0, 0)),
                  pl.BlockSpec((tn, k), lambda i, j: (j, 0))],
        out_specs=pl.BlockSpec((tm, tn), lambda i, j: (i, j)),
        out_shape=jax.ShapeDtypeStruct((m, n), BF16),
        scratch_shapes=[pltpu.VMEM((tm, k), BF16)],
        compiler_params=_params(2),
        name="in_proj",
    )(x, g.reshape(1, k), wt)


def _norm_matmul_nt_kernel(x_ref, g_ref, wt_ref, o_ref, h_ref):
    @pl.when(pl.program_id(1) == 0)
    def _():
        h_ref[...] = _rmsnorm_rows(x_ref[...].astype(F32), g_ref[...]).astype(BF16)

    o_ref[...] = lax.dot_general(wt_ref[...], h_ref[...], NT_DIMS,
                                 preferred_element_type=F32).astype(o_ref.dtype)


def _norm_matmul_nt(x, x_cols, x_col_block, g, wt, tm, tn):
    m = x.shape[0]
    n = wt.shape[0]
    return pl.pallas_call(
        _norm_matmul_nt_kernel,
        grid=(m // tm, n // tn),
        in_specs=[pl.BlockSpec((tm, x_cols), lambda i, j: (i, x_col_block)),
                  pl.BlockSpec((1, x_cols), lambda i, j: (0, 0)),
                  pl.BlockSpec((tn, x_cols), lambda i, j: (j, 0))],
        out_specs=pl.BlockSpec((tn, tm), lambda i, j: (j, i)),
        out_shape=jax.ShapeDtypeStruct((n, m), BF16),
        scratch_shapes=[pltpu.VMEM((tm, x_cols), BF16)],
        compiler_params=_params(2),
        name="norm_matmul_nt",
    )(x, g.reshape(1, x_cols), wt)


def _rope_block(blk, c1, c2):
    return blk * c1 + pltpu.roll(blk, LANE // 2, 1) * c2


def _q_proj_kernel(x_ref, g_ref, w_ref, c1_ref, c2_ref, o_ref, h_ref, *, heads_per_tile, qscale):
    @pl.when(pl.program_id(1) == 0)
    def _():
        h_ref[...] = _rmsnorm_rows(x_ref[...].astype(F32), g_ref[...]).astype(BF16)

    acc = jnp.dot(h_ref[...], w_ref[...], preferred_element_type=F32)
    c1 = c1_ref[...]
    c2 = c2_ref[...]
    for hh in range(heads_per_tile):
        lo = hh * MLA_QPAD
        o_ref[:, lo:lo + LANE] = (acc[:, lo:lo + LANE] * qscale).astype(o_ref.dtype)
        roped = _rope_block(acc[:, lo + LANE:lo + MLA_QPAD], c1, c2)
        o_ref[:, lo + LANE:lo + MLA_QPAD] = (roped * qscale).astype(o_ref.dtype)


def _q_proj(proj, g, w, c1, c2, seq, tm, tn):
    m = proj.shape[0]
    n = w.shape[1]
    s_tiles = seq // tm
    return pl.pallas_call(
        functools.partial(_q_proj_kernel, heads_per_tile=tn // MLA_QPAD,
                          qscale=LOG2E / math.sqrt(MLA_QK)),
        grid=(m // tm, n // tn),
        in_specs=[pl.BlockSpec((tm, MLA_Q_RANK), lambda i, j: (i, COL_CQ // MLA_Q_RANK)),
                  pl.BlockSpec((1, MLA_Q_RANK), lambda i, j: (0, 0)),
                  pl.BlockSpec((MLA_Q_RANK, tn), lambda i, j: (0, j)),
                  pl.BlockSpec((tm, LANE), lambda i, j: (i % s_tiles, 0)),
                  pl.BlockSpec((tm, LANE), lambda i, j: (i % s_tiles, 0))],
        out_specs=pl.BlockSpec((tm, tn), lambda i, j: (i, j)),
        out_shape=jax.ShapeDtypeStruct((m, n), BF16),
        scratch_shapes=[pltpu.VMEM((tm, MLA_Q_RANK), BF16)],
        compiler_params=_params(2),
        name="mla_q_proj",
    )(proj, g.reshape(1, MLA_Q_RANK), w, c1, c2)


def _k_rope_kernel(x_ref, c1_ref, c2_ref, o_ref):
    o_ref[...] = _rope_block(x_ref[...].astype(F32), c1_ref[...], c2_ref[...]).astype(o_ref.dtype)


def _k_rope(proj, c1, c2, seq, tm):
    m = proj.shape[0]
    s_tiles = seq // tm
    return pl.pallas_call(
        _k_rope_kernel,
        grid=(m // tm,),
        in_specs=[pl.BlockSpec((tm, LANE), lambda i: (i, COL_KPE // LANE)),
                  pl.BlockSpec((tm, LANE), lambda i: (i % s_tiles, 0)),
                  pl.BlockSpec((tm, LANE), lambda i: (i % s_tiles, 0))],
        out_specs=pl.BlockSpec((tm, LANE), lambda i: (i, 0)),
        out_shape=jax.ShapeDtypeStruct((m, LANE), BF16),
        compiler_params=_params(1),
        name="mla_k_rope",
    )(proj, c1, c2)


ONES_ROWS = 16


def _chunk_mask_t(t):
    shift = int(math.log2(CHUNK))
    kc = lax.shift_right_logical(lax.broadcasted_iota(jnp.int32, (t, t), 0), shift)
    qc = lax.shift_right_logical(lax.broadcasted_iota(jnp.int32, (t, t), 1), shift)
    return kc <= qc


def _softmax_update(s_t, vt, m_sc, acc_sc, cols=slice(None), col_max=None, shift=None):
    if col_max is None:
        col_max = jnp.max(s_t, axis=0, keepdims=True)
    if shift is not None:
        col_max = col_max + shift
    m_prev = m_sc[:, cols]
    m_new = jnp.maximum(m_prev, col_max)
    alpha = jnp.exp2(m_prev - m_new)
    m_sub = m_new if shift is None else m_new - shift
    p_t = jnp.exp2(s_t - m_sub).astype(BF16)
    v_aug = jnp.concatenate([vt, jnp.ones((ONES_ROWS, vt.shape[1]), BF16)], axis=0)
    acc_sc[:, cols] = alpha * acc_sc[:, cols] + jnp.dot(v_aug, p_t, preferred_element_type=F32)
    m_sc[:, cols] = m_new


def _attend_query_tiles(n_tiles, n_diag, stages, bufs, pairs_per_trip=1):
    assert n_diag % 2 == 0
    stages.qk(0, 0, *bufs[0])

    def tile(i, carry):
        stages.init(i)

        def pair(p):
            j = 2 * p
            stages.qk(i, j + 1, *bufs[1])
            stages.pv(i, j, *bufs[0])
            stages.qk(i, j + 2, *bufs[0])
            stages.pv(i, j + 1, *bufs[1])

        def trip(p, carry):
            for u in range(pairs_per_trip):
                pair(pairs_per_trip * p + u)
            return carry

        def single(p, carry):
            pair(p)
            return carry

        n_pairs = (n_diag // 2) * i
        n_trips = lax.div(n_pairs, pairs_per_trip)
        lax.fori_loop(0, n_trips, trip, 0)
        if pairs_per_trip > 1:
            lax.fori_loop(n_trips * pairs_per_trip, n_pairs, single, 0)
        j = n_diag * i
        for d in range(n_diag):
            if d + 1 < n_diag:
                stages.qk_diag(i, j + d + 1, d + 1, *bufs[(d + 1) % 2])
            else:
                stages.qk(jnp.minimum(i + 1, n_tiles - 1), 0, *bufs[0])
            stages.diag(i, j + d, d, *bufs[d % 2])
        stages.finalize(i)
        return carry

    lax.fori_loop(0, n_tiles, tile, 0)


class _Stages:
    def __init__(self, **fns):
        self.__dict__.update(fns)


def _mla_attn_kernel(q_ref, kn_ref, kr_ref, vt_ref, o_ref, m_sc, acc_sc, fix_sc, s0_sc, s1_sc,
                     cm0_sc, cm1_sc, *, t, tq):
    fix_sc[...] = jnp.where(_chunk_mask_t(t), 0.0, NEG_INF)

    def key_rows(j):
        return pl.ds(pl.multiple_of(j * t, t), t)

    def qry_rows(i, lo=0):
        return pl.ds(pl.multiple_of(i * tq + lo, t), tq - lo)

    def keys(j):
        rows = key_rows(j)
        return jnp.concatenate([kn_ref[rows, :], kr_ref[rows, :]], axis=1)

    def init(i):
        m_sc[...] = jnp.full(m_sc.shape, NEG_INF, F32)
        acc_sc[...] = jnp.zeros(acc_sc.shape, F32)

    def qk(i, j, s_sc, cm_sc):
        s_t = lax.dot_general(keys(j), q_ref[qry_rows(i), :], NT_DIMS, preferred_element_type=F32)
        s_sc[...] = s_t
        cm_sc[...] = jnp.max(s_t, axis=0, keepdims=True)

    def qk_diag(i, j, d, s_sc, cm_sc):
        s_sc[:, d * t:] = lax.dot_general(keys(j), q_ref[qry_rows(i, d * t), :], NT_DIMS,
                                          preferred_element_type=F32)

    def pv(i, j, s_sc, cm_sc):
        _softmax_update(s_sc[...], vt_ref[:, key_rows(j)], m_sc, acc_sc, col_max=cm_sc[...])

    def diag(i, j, d, s_sc, cm_sc):
        lo = d * t
        parts = [s_sc[:, lo:lo + t] + fix_sc[...]]
        if lo + t < tq:
            parts.append(s_sc[:, lo + t:])
        _softmax_update(jnp.concatenate(parts, axis=1), vt_ref[:, key_rows(j)], m_sc, acc_sc,
                        cols=slice(lo, tq))

    def finalize(i):
        acc = acc_sc[...]
        o_t = acc[:MLA_V] * (1.0 / acc[MLA_V:MLA_V + 1])
        o_ref[qry_rows(i), :] = o_t.T.astype(o_ref.dtype)

    stages = _Stages(init=init, qk=qk, qk_diag=qk_diag, pv=pv, diag=diag, finalize=finalize)
    _attend_query_tiles(q_ref.shape[0] // tq, tq // t, stages,
                        ((s0_sc, cm0_sc), (s1_sc, cm1_sc)), pairs_per_trip=2)


def _mla_attn(qm, kn, kr, vt, batch, seq, t, tq):
    return pl.pallas_call(
        functools.partial(_mla_attn_kernel, t=t, tq=tq),
        grid=(batch, MLA_HEADS),
        in_specs=[pl.BlockSpec((seq, MLA_QPAD), lambda b, h: (b, h)),
                  pl.BlockSpec((seq, LANE), lambda b, h: (b, h)),
                  pl.BlockSpec((seq, LANE), lambda b, h: (b, 0)),
                  pl.BlockSpec((MLA_V, seq), lambda b, h: (h, b))],
        out_specs=pl.BlockSpec((seq, MLA_V), lambda b, h: (b, h)),
        out_shape=jax.ShapeDtypeStruct((batch * seq, MLA_HEADS * MLA_V), BF16),
        scratch_shapes=[pltpu.VMEM((1, tq), F32),
                        pltpu.VMEM((MLA_V + ONES_ROWS, tq), F32),
                        pltpu.VMEM((t, t), F32),
                        pltpu.VMEM((t, tq), F32), pltpu.VMEM((t, tq), F32),
                        pltpu.VMEM((1, tq), F32), pltpu.VMEM((1, tq), F32)],
        compiler_params=_params(2),
        name="mla_attn",
    )(qm, kn, kr, vt)


def _diff_attn_kernel(slope_ref, q_ref, k_ref, vt_ref, lq1_ref, lk1_ref, lq2_ref, lk2_ref, sg_ref,
                      o_ref, m_sc, acc_sc, bias_sc, fix_sc, s0_sc, s1_sc, cm0_sc, cm1_sc, kf_sc, qf_sc,
                      *, t, tq, lambda_init):
    d = DIFF_HEAD_DIM
    slope = slope_ref[pl.program_id(1)]

    key_idx = lax.broadcasted_iota(jnp.int32, (t, t), 0)
    qry_idx = lax.broadcasted_iota(jnp.int32, (t, t), 1)
    bias_sc[...] = key_idx.astype(F32) * slope
    lane = lax.broadcasted_iota(jnp.int32, (t, LANE), 1)
    krow = lax.broadcasted_iota(jnp.int32, (t, LANE), 0)
    hi = (lax.shift_right_logical(krow, 4) * 16).astype(F32)
    lo = (krow & 15).astype(F32)
    kf_sc[...] = jnp.where(lane < 3, hi, jnp.where(lane < 6, lo, 0.0)).astype(BF16)
    sl = jnp.full((tq, LANE), slope, F32)
    h1 = sl.astype(BF16).astype(F32)
    r1 = sl - h1
    h2 = r1.astype(BF16).astype(F32)
    h3 = r1 - h2
    qlane = lax.broadcasted_iota(jnp.int32, (tq, LANE), 1)
    part = jnp.where((qlane == 0) | (qlane == 3), h1, jnp.where((qlane == 1) | (qlane == 4), h2, h3))
    qf_sc[...] = jnp.where(qlane < 6, part, 0.0).astype(BF16)
    fix = jnp.maximum(key_idx - qry_idx, 0).astype(F32) * (-2.0 * slope)
    fix_sc[...] = jnp.where(_chunk_mask_t(t), fix, NEG_INF)
    lam = (jnp.exp(jnp.sum(lq1_ref[...] * lk1_ref[...], axis=1, keepdims=True))
           - jnp.exp(jnp.sum(lq2_ref[...] * lk2_ref[...], axis=1, keepdims=True))
           + lambda_init)

    def key_rows(j):
        return pl.ds(pl.multiple_of(j * t, t), t)

    def qry_rows(i, lo=0):
        return pl.ds(pl.multiple_of(i * tq + lo, t), tq - lo)

    def scores(i, j, m, lo=0):
        lanes = slice(m * d, (m + 1) * d)
        k_aug = jnp.concatenate([k_ref[key_rows(j), lanes], kf_sc[...]], axis=1)
        q_aug = jnp.concatenate([q_ref[qry_rows(i, lo), lanes], qf_sc[lo:, :]], axis=1)
        return lax.dot_general(k_aug, q_aug, NT_DIMS, preferred_element_type=F32)

    def block_shift(i, j):
        return lax.convert_element_type(j * t - i * tq, F32) * slope

    def init(i):
        m_sc[...] = jnp.full(m_sc.shape, NEG_INF, F32)
        acc_sc[...] = jnp.zeros(acc_sc.shape, F32)

    def qk(i, j, s_sc, cm_sc):
        s_t = jnp.concatenate([scores(i, j, 0), scores(i, j, 1)], axis=1)
        s_sc[...] = s_t
        cm_sc[...] = jnp.max(s_t, axis=0, keepdims=True)

    def qk_diag(i, j, d, s_sc, cm_sc):
        for m in range(2):
            s_sc[:, m * tq + d * t:(m + 1) * tq] = scores(i, j, m, d * t)

    def pv(i, j, s_sc, cm_sc):
        _softmax_update(s_sc[...], vt_ref[:, key_rows(j)], m_sc, acc_sc, col_max=cm_sc[...],
                        shift=block_shift(i, j))

    def diag(i, j, d, s_sc, cm_sc):
        if d == 0:
            fix = fix_sc[...]
            s_t = jnp.concatenate([s_sc[:, :t] + fix, s_sc[:, t:tq], s_sc[:, tq:tq + t] + fix,
                                   s_sc[:, tq + t:]], axis=1)
            _softmax_update(s_t, vt_ref[:, key_rows(j)], m_sc, acc_sc, shift=block_shift(i, j))
            return
        for m in range(2):
            lo = m * tq + d * t
            parts = [s_sc[:, lo:lo + t] + fix_sc[...]]
            if lo + t < (m + 1) * tq:
                parts.append(s_sc[:, lo + t:(m + 1) * tq])
            _softmax_update(jnp.concatenate(parts, axis=1), vt_ref[:, key_rows(j)], m_sc, acc_sc,
                            cols=slice(lo, (m + 1) * tq), shift=block_shift(i, j))

    def finalize(i):
        acc = acc_sc[...]
        o_t = acc[:DIFF_V] * (1.0 / acc[DIFF_V:DIFF_V + 1])
        o = (o_t[:, :tq] - lam * o_t[:, tq:]).T
        y = _rmsnorm_rows(o, sg_ref[...]) * (1.0 - lambda_init)
        o_ref[qry_rows(i), :] = y.astype(o_ref.dtype)

    stages = _Stages(init=init, qk=qk, qk_diag=qk_diag, pv=pv, diag=diag, finalize=finalize)
    _attend_query_tiles(q_ref.shape[0] // tq, tq // t, stages,
                        ((s0_sc, cm0_sc), (s1_sc, cm1_sc)), pairs_per_trip=2)


def _diff_attn(proj, dvt, slopes, lq1, lk1, lq2, lk2, sub_g, batch, seq, t, tq, lambda_init):
    vec = lambda a: a.reshape(1, -1).astype(F32)
    small = lambda n: pl.BlockSpec((1, n), lambda b, h, s: (0, 0))
    grid_spec = pltpu.PrefetchScalarGridSpec(
        num_scalar_prefetch=1,
        grid=(batch, DIFF_HEADS),
        in_specs=[pl.BlockSpec((seq, DIFF_V), lambda b, h, s: (b, COL_DQ // DIFF_V + h)),
                  pl.BlockSpec((seq, DIFF_V), lambda b, h, s: (b, COL_DK // DIFF_V + h)),
                  pl.BlockSpec((DIFF_V, seq), lambda b, h, s: (h, b)),
                  small(DIFF_HEAD_DIM), small(DIFF_HEAD_DIM), small(DIFF_HEAD_DIM),
                  small(DIFF_HEAD_DIM), small(DIFF_V)],
        out_specs=pl.BlockSpec((seq, DIFF_V), lambda b, h, s: (b, h)),
        scratch_shapes=[pltpu.VMEM((1, 2 * tq), F32),
                        pltpu.VMEM((DIFF_V + ONES_ROWS, 2 * tq), F32),
                        pltpu.VMEM((t, t), F32), pltpu.VMEM((t, t), F32),
                        pltpu.VMEM((t, 2 * tq), F32), pltpu.VMEM((t, 2 * tq), F32),
                        pltpu.VMEM((1, 2 * tq), F32), pltpu.VMEM((1, 2 * tq), F32),
                        pltpu.VMEM((t, LANE), BF16), pltpu.VMEM((tq, LANE), BF16)],
    )
    return pl.pallas_call(
        functools.partial(_diff_attn_kernel, t=t, tq=tq, lambda_init=lambda_init),
        grid_spec=grid_spec,
        out_shape=jax.ShapeDtypeStruct((batch * seq, DIFF_HEADS * DIFF_V), BF16),
        compiler_params=_params(2),
        name="diff_attn",
    )(slopes, proj, proj, dvt, vec(lq1), vec(lk1), vec(lq2), vec(lk2), vec(sub_g))


def _merge_kernel(ya_ref, yb_ref, wa_ref, wb_ref, ga_ref, gb_ref, ba_ref, bb_ref, o_ref):
    a = jnp.dot(ya_ref[...], wa_ref[...], preferred_element_type=F32)
    b = jnp.dot(yb_ref[...], wb_ref[...], preferred_element_type=F32)
    g0 = jax.nn.sigmoid(ga_ref[...].astype(F32) + ba_ref[...])
    g1 = jax.nn.sigmoid(gb_ref[...].astype(F32) + bb_ref[...])
    o_ref[...] = (g0 * a + g1 * b).astype(o_ref.dtype)


def _merge(y_mla, y_diff, w_mla, w_diff, proj, b_gate, tm, tn):
    m = y_mla.shape[0]
    d = D_MODEL
    g0_blk = COL_GATE // tn
    g1_blk = (COL_GATE + d) // tn
    nb = d // tn
    bg = b_gate.reshape(1, 2 * d).astype(F32)
    return pl.pallas_call(
        _merge_kernel,
        grid=(m // tm, nb),
        in_specs=[pl.BlockSpec((tm, d), lambda i, j: (i, 0)),
                  pl.BlockSpec((tm, d), lambda i, j: (i, 0)),
                  pl.BlockSpec((d, tn), lambda i, j: (0, j)),
                  pl.BlockSpec((d, tn), lambda i, j: (0, j)),
                  pl.BlockSpec((tm, tn), lambda i, j: (i, g0_blk + j)),
                  pl.BlockSpec((tm, tn), lambda i, j: (i, g1_blk + j)),
                  pl.BlockSpec((1, tn), lambda i, j: (0, j)),
                  pl.BlockSpec((1, tn), lambda i, j: (0, nb + j))],
        out_specs=pl.BlockSpec((tm, tn), lambda i, j: (i, j)),
        out_shape=jax.ShapeDtypeStruct((m, d), BF16),
        compiler_params=_params(2),
        name="gated_merge",
    )(y_mla, y_diff, w_mla, w_diff, proj, proj, bg, bg)


def _out_proj_kernel(a_ref, w_ref, x_ref, g_ref, x1_ref, h_ref):
    y = x_ref[...] + jnp.dot(a_ref[...], w_ref[...], preferred_element_type=F32)
    x1_ref[...] = y
    h_ref[...] = _rmsnorm_rows(y, g_ref[...]).astype(h_ref.dtype)


def _out_proj(a, w, x, g, tm):
    m, k = a.shape
    n = w.shape[1]
    return pl.pallas_call(
        _out_proj_kernel,
        grid=(m // tm,),
        in_specs=[pl.BlockSpec((tm, k), lambda i: (i, 0)),
                  pl.BlockSpec((k, n), lambda i: (0, 0), pipeline_mode=pl.Buffered(1)),
                  pl.BlockSpec((tm, n), lambda i: (i, 0)),
                  pl.BlockSpec((1, n), lambda i: (0, 0))],
        out_specs=[pl.BlockSpec((tm, n), lambda i: (i, 0)),
                   pl.BlockSpec((tm, n), lambda i: (i, 0))],
        out_shape=[jax.ShapeDtypeStruct((m, n), F32), jax.ShapeDtypeStruct((m, n), BF16)],
        compiler_params=_params(1),
        name="out_proj_residual",
    )(a, w, x, g.reshape(1, n))


def _ffn_up_kernel(h_ref, wg_ref, wu_ref, o_ref, wg_sc, wu_sc):
    @pl.when(pl.program_id(1) == 0)
    def _():
        wg_sc[...] = wg_ref[...].astype(BF16)
        wu_sc[...] = wu_ref[...].astype(BF16)

    h = h_ref[...]
    gate = jnp.dot(h, wg_sc[...], preferred_element_type=F32)
    up = jnp.dot(h, wu_sc[...], preferred_element_type=F32)
    o_ref[...] = (gate * jax.nn.sigmoid(gate) * up).astype(o_ref.dtype)


def _ffn_up(h, wg, wu, tm, tn):
    m, d = h.shape
    n = wg.shape[1]
    return pl.pallas_call(
        _ffn_up_kernel,
        grid=(n // tn, m // tm),
        in_specs=[pl.BlockSpec((tm, d), lambda j, i: (i, 0)),
                  pl.BlockSpec((d, tn), lambda j, i: (0, j)),
                  pl.BlockSpec((d, tn), lambda j, i: (0, j))],
        out_specs=pl.BlockSpec((tm, tn), lambda j, i: (i, j)),
        out_shape=jax.ShapeDtypeStruct((m, n), BF16),
        scratch_shapes=[pltpu.VMEM((d, tn), BF16), pltpu.VMEM((d, tn), BF16)],
        compiler_params=_params(2),
        name="ffn_up",
    )(h, wg, wu)


def _ffn_down_kernel(a_ref, w_ref, x_ref, g_ref, o_ref, *, final_norm):
    y = x_ref[...] + jnp.dot(a_ref[...], w_ref[...], preferred_element_type=F32)
    o_ref[...] = _rmsnorm_rows(y, g_ref[...]) if final_norm else y


def _ffn_down(a, w, x, g, tm, final_norm):
    m, kdim = a.shape
    d = w.shape[1]
    return pl.pallas_call(
        functools.partial(_ffn_down_kernel, final_norm=final_norm),
        grid=(m // tm,),
        in_specs=[pl.BlockSpec((tm, kdim), lambda i: (i, 0)),
                  pl.BlockSpec((kdim, d), lambda i: (0, 0), pipeline_mode=pl.Buffered(1)),
                  pl.BlockSpec((tm, d), lambda i: (i, 0)),
                  pl.BlockSpec((1, d), lambda i: (0, 0))],
        out_specs=pl.BlockSpec((tm, d), lambda i: (i, 0)),
        out_shape=jax.ShapeDtypeStruct((m, d), F32),
        compiler_params=_params(1),
        name="ffn_down_final_norm",
    )(a, w, x, g.reshape(1, d))


W_IN_SIZES = (MLA_Q_RANK, MLA_KV_RANK, MLA_ROPE, D_MODEL, D_MODEL, D_MODEL, 2 * D_MODEL)
W_IN_OFFS = tuple(sum(W_IN_SIZES[:n]) for n in range(len(W_IN_SIZES) + 1))


def _w_in_layout_kernel(wt_ref, o_ref, dvt_ref):
    o_cq, o_ckv, o_kpe, o_dq, o_dk, o_dv, o_gate, o_end = W_IN_OFFS
    half = MLA_ROPE // 2
    cols = o_ref.shape[1]

    def cast(lo, hi):
        return wt_ref[lo:hi, :].astype(BF16)

    o_ref[COL_CQ:COL_CQ + MLA_Q_RANK, :] = cast(o_cq, o_ckv)
    o_ref[COL_KPE:COL_KPE + MLA_ROPE, :] = cast(o_kpe, o_dq)
    o_ref[COL_KPE + MLA_ROPE:COL_KPE + MLA_ROPE + half, :] = (-wt_ref[o_kpe + half:o_dq, :]).astype(BF16)
    o_ref[COL_KPE + MLA_ROPE + half:COL_KPE + 2 * MLA_ROPE, :] = cast(o_kpe, o_kpe + half)
    o_ref[COL_KPE + 2 * MLA_ROPE:COL_CKV, :] = jnp.zeros((COL_CKV - COL_KPE - 2 * MLA_ROPE, cols), BF16)
    o_ref[COL_CKV:COL_CKV + MLA_KV_RANK, :] = cast(o_ckv, o_kpe)
    o_ref[COL_CKV + MLA_KV_RANK:COL_GATE, :] = jnp.zeros((COL_GATE - COL_CKV - MLA_KV_RANK, cols), BF16)
    o_ref[COL_GATE:COL_DQ, :] = cast(o_gate, o_end)
    o_ref[COL_DQ:COL_DK, :] = (wt_ref[o_dq:o_dk, :] * (LOG2E / math.sqrt(DIFF_HEAD_DIM))).astype(BF16)
    o_ref[COL_DK:PROJ_COLS, :] = cast(o_dk, o_dv)
    dvt_ref[...] = cast(o_dv, o_gate)


def _w_in_layout(w_in_t, tk):
    n, k = w_in_t.shape
    return pl.pallas_call(
        _w_in_layout_kernel,
        grid=(k // tk,),
        in_specs=[pl.BlockSpec((n, tk), lambda i: (0, i))],
        out_specs=[pl.BlockSpec((PROJ_COLS, tk), lambda i: (0, i)),
                   pl.BlockSpec((D_MODEL, tk), lambda i: (0, i))],
        out_shape=[jax.ShapeDtypeStruct((PROJ_COLS, k), BF16),
                   jax.ShapeDtypeStruct((D_MODEL, k), BF16)],
        compiler_params=_params(1),
        name="w_in_layout",
    )(w_in_t)


def _rotate_half_cols(w):
    half = w.shape[-1] // 2
    return jnp.concatenate([-w[..., half:], w[..., :half]], axis=-1)


def _prep_w_uq(w_uq):
    w = w_uq.reshape(MLA_Q_RANK, MLA_HEADS, MLA_QK)
    nope, rope = w[..., :MLA_NOPE], w[..., MLA_NOPE:]
    w = jnp.concatenate([nope, rope, _rotate_half_cols(rope)], axis=-1)
    return w.reshape(MLA_Q_RANK, MLA_HEADS * MLA_QPAD).astype(BF16)


def _prep_w_ukv(w_ukv):
    w = w_ukv.reshape(MLA_KV_RANK, MLA_HEADS, MLA_NOPE + MLA_V)
    k_nope = w[..., :MLA_NOPE].reshape(MLA_KV_RANK, MLA_HEADS * MLA_NOPE)
    v = w[..., MLA_NOPE:].reshape(MLA_KV_RANK, MLA_HEADS * MLA_V)
    return k_nope.astype(BF16), v.T.astype(BF16)


def _rope_tables(seq):
    inv = ROPE_THETA ** (-jnp.arange(0, MLA_ROPE, 2, dtype=F32) / MLA_ROPE)
    ang = jnp.arange(seq, dtype=F32)[:, None] * inv[None, :]
    cos, sin = jnp.cos(ang), jnp.sin(ang)
    zeros = jnp.zeros((seq, LANE - MLA_ROPE), F32)
    return (jnp.concatenate([cos, cos, zeros], axis=1), jnp.concatenate([sin, sin, zeros], axis=1))


def kernel(x, attn_norm_g, w_in, b_gate, q_norm_g, w_uq, kv_norm_g, w_ukv, lambda_q1, lambda_k1,
           lambda_q2, lambda_k2, diff_norm_g, w_mla_proj, w_diff_proj, w_out, ffn_norm_g,
           w_ffn_gate, w_ffn_up, w_ffn_down, final_norm_g):
    batch, seq, d = x.shape
    depth = w_in.shape[0]
    assert d == D_MODEL and seq % MLA_QUERY_TILE == 0 and seq % ROW_TILE == 0
    tokens = batch * seq
    xt = x.reshape(tokens, d)
    c1, c2 = _rope_tables(seq)
    slopes = jnp.exp2(-8.0 * jnp.arange(1, DIFF_HEADS + 1, dtype=F32) / DIFF_HEADS) * LOG2E
    tm, t = ROW_TILE, ATTN_KEY_BLOCK
    ckv_block = COL_CKV // MLA_KV_RANK

    for layer in range(depth):
        lambda_init = 0.8 - 0.6 * math.exp(-0.3 * layer)
        w1_t, w_dv_t = _w_in_layout(w_in[layer].T, LAYOUT_COL_TILE)
        w_kn, w_v_t = _prep_w_ukv(w_ukv[layer])
        proj = _norm_matmul_wt(xt, attn_norm_g[layer], w1_t, tm, PROJ_COL_TILE)
        dvt = _norm_matmul_nt(xt, d, 0, attn_norm_g[layer], w_dv_t, tm, PROJ_COL_TILE)
        qm = _q_proj(proj, q_norm_g[layer], _prep_w_uq(w_uq[layer]), c1, c2, seq, tm,
                     PROJ_COL_TILE)
        kn = _norm_matmul(proj, MLA_KV_RANK, ckv_block, kv_norm_g[layer], w_kn, tm, PROJ_COL_TILE)
        vt = _norm_matmul_nt(proj, MLA_KV_RANK, ckv_block, kv_norm_g[layer], w_v_t, tm,
                             PROJ_COL_TILE)
        kr = _k_rope(proj, c1, c2, seq, tm)
        y_mla = _mla_attn(qm, kn, kr, vt, batch, seq, t, MLA_QUERY_TILE)
        y_diff = _diff_attn(proj, dvt, slopes, lambda_q1[layer], lambda_k1[layer], lambda_q2[layer],
                            lambda_k2[layer], diff_norm_g[layer], batch, seq, t, DIFF_QUERY_TILE,
                            lambda_init)
        merged = _merge(y_mla, y_diff, w_mla_proj[layer].astype(BF16),
                        w_diff_proj[layer].astype(BF16), proj, b_gate[layer], tm, MERGE_COL_TILE)
        xt, h_ffn = _out_proj(merged, w_out[layer].astype(BF16), xt, ffn_norm_g[layer],
                              ROW_TILE_RESIDENT_W)
        a = _ffn_up(h_ffn, w_ffn_gate[layer], w_ffn_up[layer], tm, FFN_COL_TILE)
        xt = _ffn_down(a, w_ffn_down[layer].astype(BF16), xt, final_norm_g, ROW_TILE_RESIDENT_W,
                       final_norm=layer == depth - 1)
    return xt.reshape(batch, seq, d)
```
